```python
import jax, jax.numpy as jnp
from jax import lax
import numpy as np

D_MODEL = 2048
BATCH = 4
SEQ = 2048
DEPTH = 4

N_META = 16
BLOCK = 128
EPS = 1e-6
NEG_INF = -1e30
ROPE_THETA = 10000.0
MLA_HEADS = 8
MLA_Q_RANK = 512
MLA_KV_RANK = 512
MLA_NOPE = 128
MLA_ROPE = 64
MLA_V = 128
CONV_WIDTH = 1024
CONV_K = 3
FOX_HEADS = 8
FOX_HEAD_DIM = 128
FORGET_BIAS_MEAN = 2.0
N_BRANCH = 3
BRANCH_WIDTH = 1024
D_FF = -(-8 * D_MODEL // (3 * 256)) * 256
FOX_WIDTH = FOX_HEADS * FOX_HEAD_DIM
IN_SPLITS = (MLA_Q_RANK, MLA_KV_RANK, MLA_ROPE,
             CONV_WIDTH, CONV_WIDTH, CONV_WIDTH,
             FOX_WIDTH, FOX_WIDTH, FOX_WIDTH, FOX_HEADS,
             N_BRANCH * D_MODEL)
D_IN = sum(IN_SPLITS)

kernel_name = "hybrid_mla_conv_fox_gated_block"


def _split_points():
    return [int(v) for v in np.cumsum(IN_SPLITS)[:-1]]


def rms_norm(x, g):
    xf = x.astype(jnp.float32)
    y = xf * lax.rsqrt(jnp.mean(xf * xf, axis=-1, keepdims=True) + EPS) * g.astype(jnp.float32)
    return y.astype(x.dtype)


def rope_tables(length):
    inv_freq = 1.0 / (ROPE_THETA ** (jnp.arange(0, MLA_ROPE, 2, dtype=jnp.float32) / MLA_ROPE))
    ang = jnp.arange(length, dtype=jnp.float32)[:, None] * inv_freq[None, :]
    return jnp.cos(ang)[:, None, :], jnp.sin(ang)[:, None, :]


def apply_rope(x, cos, sin):
    xf = x.astype(jnp.float32)
    x1, x2 = xf[..., : MLA_ROPE // 2], xf[..., MLA_ROPE // 2:]
    return jnp.concatenate([x1 * cos - x2 * sin, x1 * sin + x2 * cos], axis=-1).astype(x.dtype)


def to_heads(t, n_heads):
    b, l, _ = t.shape
    return t.reshape(b, l, n_heads, -1).transpose(0, 2, 1, 3)


def blocked_causal_attention(q, k, v, scale, decay=None):
    b, h, l, _ = q.shape
    pad = (-l) % BLOCK
    padw = ((0, 0), (0, 0), (pad, 0), (0, 0))
    qp, kp, vp = jnp.pad(q, padw), jnp.pad(k, padw), jnp.pad(v, padw)
    lp = l + pad
    nb = lp // BLOCK
    kpos = jnp.arange(lp)
    key_ok = kpos >= pad
    q_blocks = qp.reshape(b, h, nb, BLOCK, -1).transpose(2, 0, 1, 3, 4)
    dp = None if decay is None else jnp.pad(decay, ((0, 0), (0, 0), (pad, 0)))

    def attend(qi, i, di):
        s = jnp.einsum('bhqd,bhkd->bhqk', qi, kp, preferred_element_type=jnp.float32) * scale
        if di is not None:
            s = s + (di[..., :, None] - dp[:, :, None, :])
        qpos = i * BLOCK + jnp.arange(BLOCK)
        mask = (kpos[None, :] <= qpos[:, None]) & key_ok[None, :]
        p = jax.nn.softmax(jnp.where(mask, s, NEG_INF), axis=-1).astype(vp.dtype)
        return jnp.einsum('bhqk,bhkd->bhqd', p, vp)

    idx = jnp.arange(nb)
    if decay is None:
        out = lax.map(lambda a: attend(a[0], a[1], None), (q_blocks, idx))
    else:
        d_blocks = dp.reshape(b, h, nb, BLOCK).transpose(2, 0, 1, 3)
        out = lax.map(lambda a: attend(a[0], a[1], a[2]), (q_blocks, idx, d_blocks))
    out = out.transpose(1, 2, 0, 3, 4).reshape(b, h, lp, -1)
    return out[:, :, pad:]


def hybrid_mixer(h, w_in, b_forget, g_q_lat, g_kv_lat, w_uq, w_ukv, conv_w, w_branch, w_out, cos, sin):
    b, l, _ = h.shape
    proj = h @ w_in
    (c_q, c_kv, k_pe, conv_b, conv_c, conv_x,
     f_q, f_k, f_v, f_logit, gate_logit) = jnp.split(proj, _split_points(), axis=-1)

    q = (rms_norm(c_q, g_q_lat) @ w_uq).reshape(b, l, MLA_HEADS, MLA_NOPE + MLA_ROPE)
    q_nope, q_pe = q[..., :MLA_NOPE], apply_rope(q[..., MLA_NOPE:], cos, sin)
    kv = (rms_norm(c_kv, g_kv_lat) @ w_ukv).reshape(b, l, MLA_HEADS, MLA_NOPE + MLA_V)
    k_nope, v_a = kv[..., :MLA_NOPE], kv[..., MLA_NOPE:]
    k_pe = apply_rope(k_pe[:, :, None, :], cos, sin)
    q_a = jnp.concatenate([q_nope, q_pe], axis=-1)
    k_a = jnp.concatenate([k_nope, jnp.broadcast_to(k_pe, (b, l, MLA_HEADS, MLA_ROPE))], axis=-1)
    o_a = blocked_causal_attention(q_a.transpose(0, 2, 1, 3), k_a.transpose(0, 2, 1, 3),
                                   v_a.transpose(0, 2, 1, 3), (MLA_NOPE + MLA_ROPE) ** -0.5)
    o_a = o_a.transpose(0, 2, 1, 3).reshape(b, l, MLA_HEADS * MLA_V)

    u = conv_c * conv_x
    u = lax.conv_general_dilated(u, conv_w[:, None, :].astype(u.dtype), window_strides=(1,),
                                 padding=[(CONV_K - 1, 0)], dimension_numbers=('NWC', 'WIO', 'NWC'),
                                 feature_group_count=CONV_WIDTH)
    o_b = conv_b * u

    log_f = jax.nn.log_sigmoid(f_logit.astype(jnp.float32) + b_forget.astype(jnp.float32))
    c = jnp.cumsum(log_f, axis=1).transpose(0, 2, 1)
    o_c = blocked_causal_attention(to_heads(f_q, FOX_HEADS), to_heads(f_k, FOX_HEADS),
                                   to_heads(f_v, FOX_HEADS), FOX_HEAD_DIM ** -0.5, decay=c)
    o_c = o_c.transpose(0, 2, 1, 3).reshape(b, l, FOX_WIDTH)

    o = jnp.stack([o_a, o_b, o_c], axis=2)
    y = jnp.einsum('blnw,nwd->blnd', o, w_branch)
    gates = jax.nn.sigmoid(gate_logit.astype(jnp.float32)).astype(h.dtype).reshape(b, l, N_BRANCH, D_MODEL)
    merged = jnp.sum(gates * y, axis=2)
    return merged @ w_out


def swiglu(h, w_ffn_in, w_ffn_out):
    g, u = jnp.split(h @ w_ffn_in, 2, axis=-1)
    return (jax.nn.silu(g) * u) @ w_ffn_out


def setup_inputs(seed: int = 0) -> dict:
    key = jax.random.key(seed)
    ks = jax.random.split(key, 18)
    f32 = jnp.float32

    def dense(k, shape, fan_in):
        return jax.random.normal(k, shape, f32) * fan_in ** -0.5

    def gain(k, shape):
        return 1.0 + 0.05 * jax.random.normal(k, shape, f32)

    return {
        "x": jax.random.normal(ks[0], (BATCH, SEQ, D_MODEL), f32),
        "meta": jax.random.normal(ks[1], (N_META, D_MODEL), f32),
        "w_in": dense(ks[2], (DEPTH, D_MODEL, D_IN), D_MODEL),
        "b_forget": FORGET_BIAS_MEAN + 0.1 * jax.random.normal(ks[3], (DEPTH, FOX_HEADS), f32),
        "g_q_lat": gain(ks[4], (DEPTH, MLA_Q_RANK)),
        "g_kv_lat": gain(ks[5], (DEPTH, MLA_KV_RANK)),
        "w_uq": dense(ks[6], (DEPTH, MLA_Q_RANK, MLA_HEADS * (MLA_NOPE + MLA_ROPE)), MLA_Q_RANK),
        "w_ukv": dense(ks[7], (DEPTH, MLA_KV_RANK, MLA_HEADS * (MLA_NOPE + MLA_V)), MLA_KV_RANK),
        "conv_w": dense(ks[8], (DEPTH, CONV_K, CONV_WIDTH), CONV_K),
        "w_branch": dense(ks[9], (DEPTH, N_BRANCH, BRANCH_WIDTH, D_MODEL), BRANCH_WIDTH),
        "w_out": dense(ks[10], (DEPTH, D_MODEL, D_MODEL), D_MODEL),
        "w_ffn_in": dense(ks[11], (DEPTH, D_MODEL, 2 * D_FF), D_MODEL),
        "w_ffn_out": dense(ks[12], (DEPTH, D_FF, D_MODEL), D_FF),
        "g_mix_pre": gain(ks[13], (DEPTH, D_MODEL)),
        "g_mix_post": gain(ks[14], (DEPTH, D_MODEL)),
        "g_ffn_pre": gain(ks[15], (DEPTH, D_MODEL)),
        "g_ffn_post": gain(ks[16], (DEPTH, D_MODEL)),
    }


def reference(x, meta, w_in, b_forget, g_q_lat, g_kv_lat, w_uq, w_ukv, conv_w, w_branch, w_out,
              w_ffn_in, w_ffn_out, g_mix_pre, g_mix_post, g_ffn_pre, g_ffn_post):
    b, s, _ = x.shape
    length = N_META + s
    h = jnp.concatenate([jnp.broadcast_to(meta[None].astype(x.dtype), (b, N_META, D_MODEL)), x], axis=1)
    cos, sin = rope_tables(length)
    for layer in range(DEPTH):
        hn = rms_norm(h, g_mix_pre[layer])
        mix = hybrid_mixer(hn, w_in[layer], b_forget[layer], g_q_lat[layer], g_kv_lat[layer],
                           w_uq[layer], w_ukv[layer], conv_w[layer], w_branch[layer], w_out[layer], cos, sin)
        h = h + rms_norm(mix, g_mix_post[layer])
        hn = rms_norm(h, g_ffn_pre[layer])
        h = h + rms_norm(swiglu(hn, w_ffn_in[layer], w_ffn_out[layer]), g_ffn_post[layer])
    return h[:, N_META:]
```

```python
import functools

import jax
import jax.numpy as jnp
from jax import lax
from jax.experimental import pallas as pl
from jax.experimental.pallas import tpu as pltpu

D_MODEL = 2048
N_META = 16
EPS = 1e-6
NEG_INF = -1e30
ROPE_THETA = 10000.0
MLA_HEADS = 8
MLA_Q_RANK = 512
MLA_KV_RANK = 512
MLA_NOPE = 128
MLA_ROPE = 64
MLA_V = 128
CONV_WIDTH = 1024
CONV_K = 3
FOX_HEADS = 8
FOX_HEAD_DIM = 128
N_BRANCH = 3
BRANCH_WIDTH = 1024
D_FF = 5632

MLA_SCALE = (MLA_NOPE + MLA_ROPE) ** -0.5
FOX_SCALE = FOX_HEAD_DIM ** -0.5

LANES = 128
VMEM_LIMIT_BYTES = 56 * 1024 * 1024

ATT_BLOCK = 128
PAD_FRONT = (-N_META) % ATT_BLOCK

IN_TILE = 1024
N_GATE_TILES = N_BRANCH * D_MODEL // IN_TILE
TILE_LAT = 6
TILE_CONV_B, TILE_CONV_C, TILE_CONV_X = 7, 8, 9
TILE_FQ, TILE_FK, TILE_FV = 10, 11, 12
N_IN_TILES = 13
SMALL_W = 256


def _cparams(semantics):
    return pltpu.CompilerParams(dimension_semantics=semantics, vmem_limit_bytes=VMEM_LIMIT_BYTES)


def _rms(x, g):
    ms = jnp.mean(x * x, axis=-1, keepdims=True)
    return x * lax.rsqrt(ms + EPS) * g


def _prenorm_kernel(h_ref, g_ref, o_ref):
    o_ref[...] = _rms(h_ref[...], g_ref[...]).astype(o_ref.dtype)


def _prenorm(h, g, tm):
    tp = h.shape[0]
    return pl.pallas_call(
        _prenorm_kernel,
        out_shape=jax.ShapeDtypeStruct((tp, D_MODEL), jnp.bfloat16),
        grid=(tp // tm,),
        in_specs=[pl.BlockSpec((tm, D_MODEL), lambda i: (i, 0)),
                  pl.BlockSpec((1, D_MODEL), lambda i: (0, 0))],
        out_specs=pl.BlockSpec((tm, D_MODEL), lambda i: (i, 0)),
        compiler_params=_cparams(("parallel",)),
        name="prenorm",
    )(h, g)


def _inproj_kernel(x_ref, w_ref, o_ref):
    j = pl.program_id(0)
    acc = jnp.dot(x_ref[...], w_ref[...], preferred_element_type=jnp.float32)
    is_gate = j < N_GATE_TILES
    is_fq = j == TILE_FQ

    @pl.when(is_gate)
    def _():
        o_ref[...] = (1.0 / (1.0 + jnp.exp(-acc))).astype(o_ref.dtype)

    @pl.when(is_fq)
    def _():
        o_ref[...] = (acc * FOX_SCALE).astype(o_ref.dtype)

    @pl.when(jnp.logical_not(jnp.logical_or(is_gate, is_fq)))
    def _():
        o_ref[...] = acc.astype(o_ref.dtype)


def _inproj(xn, w_main, tm):
    tp = xn.shape[0]
    return pl.pallas_call(
        _inproj_kernel,
        out_shape=jax.ShapeDtypeStruct((tp, N_IN_TILES * IN_TILE), jnp.bfloat16),
        grid=(N_IN_TILES, tp // tm),
        in_specs=[pl.BlockSpec((tm, D_MODEL), lambda j, i: (i, 0)),
                  pl.BlockSpec((D_MODEL, IN_TILE), lambda j, i: (0, j))],
        out_specs=pl.BlockSpec((tm, IN_TILE), lambda j, i: (i, j)),
        compiler_params=_cparams(("parallel", "parallel")),
        name="inproj",
    )(xn, w_main)


def _smallproj_kernel(x_ref, w_ref, o_ref):
    o_ref[...] = jnp.dot(x_ref[...], w_ref[...], preferred_element_type=jnp.float32)


def _smallproj(xn, w_small, tm):
    tp = xn.shape[0]
    return pl.pallas_call(
        _smallproj_kernel,
        out_shape=jax.ShapeDtypeStruct((tp, SMALL_W), jnp.float32),
        grid=(tp // tm,),
        in_specs=[pl.BlockSpec((tm, D_MODEL), lambda i: (i, 0)),
                  pl.BlockSpec((D_MODEL, SMALL_W), lambda i: (0, 0))],
        out_specs=pl.BlockSpec((tm, SMALL_W), lambda i: (i, 0)),
        compiler_params=_cparams(("parallel",)),
        name="smallproj",
    )(xn, w_small)


def _rope_pair(v, cos_t, sin_t):
    return v * cos_t + pltpu.roll(v, 64, 1) * sin_t


def _mla_latent_kernel(lat_ref, small_ref, gq_ref, gkv_ref, wq_ref, wkv_ref, cos_ref, sin_ref,
                       qm_ref, qe_ref, km_ref, ke_ref, v_ref):
    lat = lat_ref[...].astype(jnp.float32)
    cos_t = cos_ref[...]
    sin_t = sin_ref[...]
    cq = _rms(lat[:, :MLA_Q_RANK], gq_ref[...]).astype(jnp.bfloat16)
    ckv = _rms(lat[:, MLA_Q_RANK:], gkv_ref[...]).astype(jnp.bfloat16)
    q = jnp.dot(cq, wq_ref[...], preferred_element_type=jnp.float32)
    for h in range(MLA_HEADS):
        base = h * 2 * LANES
        qm_ref[:, h * LANES:(h + 1) * LANES] = (q[:, base:base + LANES] * MLA_SCALE).astype(qm_ref.dtype)
        roped = _rope_pair(q[:, base + LANES:base + 2 * LANES], cos_t, sin_t)
        qe_ref[:, h * LANES:(h + 1) * LANES] = (roped * MLA_SCALE).astype(qe_ref.dtype)
    kv = jnp.dot(ckv, wkv_ref[...], preferred_element_type=jnp.float32)
    km_ref[...] = kv[:, :MLA_HEADS * MLA_NOPE].astype(km_ref.dtype)
    v_ref[...] = kv[:, MLA_HEADS * MLA_NOPE:].astype(v_ref.dtype)
    ke_ref[...] = _rope_pair(small_ref[...], cos_t, sin_t).astype(ke_ref.dtype)


def _mla_latent(proj, small, gq, gkv, wq, wkv, cos_t, sin_t, lp, tm):
    tp = proj.shape[0]
    per_batch = lp // tm
    hw = MLA_HEADS * LANES
    bf = jnp.bfloat16
    row = lambda i: (i, 0)
    const = lambda i: (0, 0)
    return pl.pallas_call(
        _mla_latent_kernel,
        out_shape=(jax.ShapeDtypeStruct((tp, hw), bf), jax.ShapeDtypeStruct((tp, hw), bf),
                   jax.ShapeDtypeStruct((tp, hw), bf), jax.ShapeDtypeStruct((tp, LANES), bf),
                   jax.ShapeDtypeStruct((tp, hw), bf)),
        grid=(tp // tm,),
        in_specs=[pl.BlockSpec((tm, IN_TILE), lambda i: (i, TILE_LAT)),
                  pl.BlockSpec((tm, LANES), row),
                  pl.BlockSpec((1, MLA_Q_RANK), const),
                  pl.BlockSpec((1, MLA_KV_RANK), const),
                  pl.BlockSpec((MLA_Q_RANK, 2 * hw), const),
                  pl.BlockSpec((MLA_KV_RANK, 2 * hw), const),
                  pl.BlockSpec((tm, LANES), lambda i: (i % per_batch, 0)),
                  pl.BlockSpec((tm, LANES), lambda i: (i % per_batch, 0))],
        out_specs=(pl.BlockSpec((tm, hw), row), pl.BlockSpec((tm, hw), row),
                   pl.BlockSpec((tm, hw), row), pl.BlockSpec((tm, LANES), row),
                   pl.BlockSpec((tm, hw), row)),
        compiler_params=_cparams(("parallel",)),
        name="mla_latent",
    )(proj, small, gq, gkv, wq, wkv, cos_t, sin_t)


def _split3(x):
    hi = x.astype(jnp.bfloat16)
    r1 = x - hi.astype(jnp.float32)
    mid = r1.astype(jnp.bfloat16)
    lo = (r1 - mid.astype(jnp.float32)).astype(jnp.bfloat16)
    return hi, mid, lo


def _fox_decay_kernel(fl_ref, bias_ref, tri_ref, selq_ref, selk_ref, aq_ref, ak_ref, carry_ref):
    blk = pl.program_id(1)

    @pl.when(blk == 0)
    def _():
        carry_ref[...] = jnp.zeros_like(carry_ref)

    z = fl_ref[...] + bias_ref[...]
    log_f = jnp.minimum(z, 0.0) - jnp.log1p(jnp.exp(-jnp.abs(z)))
    rows = lax.broadcasted_iota(jnp.int32, log_f.shape, 0) + blk * ATT_BLOCK
    log_f = jnp.where(rows >= PAD_FRONT, log_f, 0.0)
    tri = tri_ref[...]
    c = carry_ref[0:1, :]
    for part in _split3(log_f):
        c = c + jnp.dot(tri, part, preferred_element_type=jnp.float32)
    carry_ref[...] = jnp.broadcast_to(c[ATT_BLOCK - 1:ATT_BLOCK, :], carry_ref.shape)
    parts = jnp.concatenate(_split3(c), axis=1)
    lane = lax.broadcasted_iota(jnp.int32, aq_ref.shape, 1) % LANES
    aq = jnp.dot(parts, selq_ref[...], preferred_element_type=jnp.float32)
    ak = jnp.dot(parts, selk_ref[...], preferred_element_type=jnp.float32)
    aq_ref[...] = jnp.where((lane >= 3) & (lane < 6), 1.0, aq).astype(aq_ref.dtype)
    ak_ref[...] = jnp.where(lane < 3, 1.0, ak).astype(ak_ref.dtype)


def _fox_decay(small, bias, tri, selq, selk, batch, lp):
    tp = small.shape[0]
    nblk = lp // ATT_BLOCK
    hw = FOX_HEADS * LANES
    const = lambda b, i: (0, 0)
    row = lambda b, i: (b * nblk + i, 0)
    return pl.pallas_call(
        _fox_decay_kernel,
        out_shape=(jax.ShapeDtypeStruct((tp, hw), jnp.bfloat16),
                   jax.ShapeDtypeStruct((tp, hw), jnp.bfloat16)),
        grid=(batch, nblk),
        in_specs=[pl.BlockSpec((ATT_BLOCK, LANES), lambda b, i: (b * nblk + i, 1)),
                  pl.BlockSpec((1, LANES), const),
                  pl.BlockSpec((ATT_BLOCK, ATT_BLOCK), const),
                  pl.BlockSpec((3 * LANES, hw), const),
                  pl.BlockSpec((3 * LANES, hw), const)],
        out_specs=(pl.BlockSpec((ATT_BLOCK, hw), row), pl.BlockSpec((ATT_BLOCK, hw), row)),
        scratch_shapes=[pltpu.VMEM((8, LANES), jnp.float32)],
        compiler_params=_cparams(("parallel", "arbitrary")),
        name="fox_decay",
    )(small, bias, tri, selq, selk)


def _attn_kernel(qm_ref, qe_ref, km_ref, ke_ref, v_ref, o_ref, *, nblk):
    blk = ATT_BLOCK

    def q_body(qi, _):
        q0 = pl.multiple_of(qi * blk, blk)
        q = jnp.concatenate([qm_ref[pl.ds(q0, blk), :], qe_ref[pl.ds(q0, blk), :]], axis=1)
        row_pos = q0 + lax.broadcasted_iota(jnp.int32, (blk, blk), 0)

        def kv_body(kj, carry):
            m, l, acc = carry
            k0 = pl.multiple_of(kj * blk, blk)
            k = jnp.concatenate([km_ref[pl.ds(k0, blk), :], ke_ref[pl.ds(k0, blk), :]], axis=1)
            s = lax.dot_general(q, k, (((1,), (1,)), ((), ())), preferred_element_type=jnp.float32)
            col_pos = k0 + lax.broadcasted_iota(jnp.int32, (blk, blk), 1)
            ok = (col_pos <= row_pos) & (col_pos >= PAD_FRONT)
            s = jnp.where(ok, s, NEG_INF)
            m_new = jnp.maximum(m, jnp.max(s, axis=1, keepdims=True))
            alpha = jnp.exp(m - m_new)
            p = jnp.exp(s - m_new)
            l_new = alpha * l + jnp.sum(p, axis=1, keepdims=True)
            pv = jnp.dot(p.astype(jnp.bfloat16), v_ref[pl.ds(k0, blk), :],
                         preferred_element_type=jnp.float32)
            return m_new, l_new, alpha * acc + pv

        init = (jnp.full((blk, 1), NEG_INF, jnp.float32), jnp.zeros((blk, 1), jnp.float32),
                jnp.zeros((blk, v_ref.shape[1]), jnp.float32))
        m, l, acc = lax.fori_loop(0, qi + 1, kv_body, init)
        o_ref[pl.ds(q0, blk), :] = (acc / l).astype(o_ref.dtype)
        return 0

    lax.fori_loop(0, nblk, q_body, 0)


def _attention(qm, qe, km, ke, v, batch, lp, n_heads, shared_ke, qm_col=0, km_col=0, v_col=0):
    tp = qm.shape[0]
    head = lambda b, h: (b, h)
    at = lambda col: (lambda b, h: (b, col + h))
    ke_map = (lambda b, h: (b, 0)) if shared_ke else head
    blockspec = lambda m: pl.BlockSpec((lp, LANES), m)
    return pl.pallas_call(
        functools.partial(_attn_kernel, nblk=lp // ATT_BLOCK),
        out_shape=jax.ShapeDtypeStruct((tp, n_heads * LANES), jnp.bfloat16),
        grid=(batch, n_heads),
        in_specs=[blockspec(at(qm_col)), blockspec(head), blockspec(at(km_col)), blockspec(ke_map),
                  blockspec(at(v_col))],
        out_specs=blockspec(head),
        compiler_params=_cparams(("parallel", "parallel")),
        name="attention",
    )(qm, qe, km, ke, v)


def _conv_kernel(b_ref, c_ref, x_ref, w_ref, o_ref, tail_ref):
    i = pl.program_id(1)
    tm = o_ref.shape[0]

    @pl.when(i == 0)
    def _():
        tail_ref[...] = jnp.zeros_like(tail_ref)

    u = c_ref[...].astype(jnp.float32) * x_ref[...].astype(jnp.float32)
    rows = lax.broadcasted_iota(jnp.int32, u.shape, 0)
    u = jnp.where(rows + i * tm >= PAD_FRONT, u, 0.0)
    prev1 = tail_ref[7:8, :]
    prev2 = tail_ref[6:7, :]
    u1 = jnp.where(rows == 0, prev1, pltpu.roll(u, 1, 0))
    u2 = jnp.where(rows == 0, prev2, jnp.where(rows == 1, prev1, pltpu.roll(u, 2, 0)))
    w = w_ref[...]
    conv = w[2:3, :] * u + w[1:2, :] * u1 + w[0:1, :] * u2
    o_ref[...] = (b_ref[...].astype(jnp.float32) * conv).astype(o_ref.dtype)
    tail_ref[...] = u[tm - 8:tm, :]


def _gated_conv(proj, conv_w, batch, lp, tm):
    tp = proj.shape[0]
    per_batch = lp // tm
    col = lambda t: (lambda b, i: (b * per_batch + i, t))
    return pl.pallas_call(
        _conv_kernel,
        out_shape=jax.ShapeDtypeStruct((tp, CONV_WIDTH), jnp.bfloat16),
        grid=(batch, per_batch),
        in_specs=[pl.BlockSpec((tm, IN_TILE), col(TILE_CONV_B)),
                  pl.BlockSpec((tm, IN_TILE), col(TILE_CONV_C)),
                  pl.BlockSpec((tm, IN_TILE), col(TILE_CONV_X)),
                  pl.BlockSpec((8, CONV_WIDTH), lambda b, i: (0, 0))],
        out_specs=pl.BlockSpec((tm, CONV_WIDTH), lambda b, i: (b * per_batch + i, 0)),
        scratch_shapes=[pltpu.VMEM((8, CONV_WIDTH), jnp.float32)],
        compiler_params=_cparams(("parallel", "arbitrary")),
        name="gated_conv",
    )(proj, proj, proj, conv_w)


def _residual_epilogue(y, h_ref, gpost_ref, gnext_ref, ho_ref, hn_ref):
    h_new = h_ref[...] + _rms(y, gpost_ref[...])
    ho_ref[...] = h_new
    if hn_ref is not None:
        hn_ref[...] = _rms(h_new, gnext_ref[...]).astype(hn_ref.dtype)


def _merge_out_kernel(oa_ref, ob_ref, oc_ref, gate_ref, wb_ref, wo_ref, h_ref, gpost_ref, gnext_ref,
                      ho_ref, hn_ref):
    merged = None
    for n, o_ref in enumerate((oa_ref, ob_ref, oc_ref)):
        y = jnp.dot(o_ref[...], wb_ref[n], preferred_element_type=jnp.float32)
        term = gate_ref[:, n * D_MODEL:(n + 1) * D_MODEL].astype(jnp.float32) * y
        merged = term if merged is None else merged + term
    mix = jnp.dot(merged.astype(jnp.bfloat16), wo_ref[...], preferred_element_type=jnp.float32)
    _residual_epilogue(mix, h_ref, gpost_ref, gnext_ref, ho_ref, hn_ref)


def _merge_out(o_a, o_b, o_c, proj, wb, wo, h, gpost, gnext, tm):
    tp = h.shape[0]
    row = lambda i: (i, 0)
    const = lambda i: (0, 0)
    single = pl.Buffered(1)
    return pl.pallas_call(
        _merge_out_kernel,
        out_shape=(jax.ShapeDtypeStruct((tp, D_MODEL), jnp.float32),
                   jax.ShapeDtypeStruct((tp, D_MODEL), jnp.bfloat16)),
        grid=(tp // tm,),
        in_specs=[pl.BlockSpec((tm, BRANCH_WIDTH), row),
                  pl.BlockSpec((tm, BRANCH_WIDTH), row),
                  pl.BlockSpec((tm, BRANCH_WIDTH), row),
                  pl.BlockSpec((tm, N_BRANCH * D_MODEL), lambda i: (i, 0)),
                  pl.BlockSpec((N_BRANCH, BRANCH_WIDTH, D_MODEL), lambda i: (0, 0, 0), pipeline_mode=single),
                  pl.BlockSpec((D_MODEL, D_MODEL), const, pipeline_mode=single),
                  pl.BlockSpec((tm, D_MODEL), row),
                  pl.BlockSpec((1, D_MODEL), const),
                  pl.BlockSpec((1, D_MODEL), const)],
        out_specs=(pl.BlockSpec((tm, D_MODEL), row), pl.BlockSpec((tm, D_MODEL), row)),
        compiler_params=_cparams(("parallel",)),
        name="merge_out",
    )(o_a, o_b, o_c, proj, wb, wo, h, gpost, gnext)


FF_TILE = 512


def _ffn_in_kernel(x_ref, wg_ref, wu_ref, o_ref):
    x = x_ref[...]
    g = jnp.dot(x, wg_ref[...], preferred_element_type=jnp.float32)
    u = jnp.dot(x, wu_ref[...], preferred_element_type=jnp.float32)
    o_ref[...] = (g * (1.0 / (1.0 + jnp.exp(-g))) * u).astype(o_ref.dtype)


def _ffn_in(xn, w, tm):
    tp = xn.shape[0]
    n_tiles = D_FF // FF_TILE
    return pl.pallas_call(
        _ffn_in_kernel,
        out_shape=jax.ShapeDtypeStruct((tp, D_FF), jnp.bfloat16),
        grid=(n_tiles, tp // tm),
        in_specs=[pl.BlockSpec((tm, D_MODEL), lambda j, i: (i, 0)),
                  pl.BlockSpec((D_MODEL, FF_TILE), lambda j, i: (0, j)),
                  pl.BlockSpec((D_MODEL, FF_TILE), lambda j, i: (0, j + n_tiles))],
        out_specs=pl.BlockSpec((tm, FF_TILE), lambda j, i: (i, j)),
        compiler_params=_cparams(("parallel", "parallel")),
        name="ffn_in",
    )(xn, w, w)


def _ffn_out_kernel(a_ref, w_ref, h_ref, gpost_ref, gnext_ref, ho_ref, hn_ref):
    y = jnp.dot(a_ref[...], w_ref[...], preferred_element_type=jnp.float32)
    _residual_epilogue(y, h_ref, gpost_ref, gnext_ref, ho_ref, hn_ref)


def _ffn_out_last_kernel(a_ref, w_ref, h_ref, gpost_ref, ho_ref):
    y = jnp.dot(a_ref[...], w_ref[...], preferred_element_type=jnp.float32)
    _residual_epilogue(y, h_ref, gpost_ref, None, ho_ref, None)


def _ffn_out(act, w, h, gpost, gnext, tm):
    tp = h.shape[0]
    row = lambda i: (i, 0)
    const = lambda i: (0, 0)
    in_specs = [pl.BlockSpec((tm, D_FF), row),
                pl.BlockSpec((D_FF, D_MODEL), const, pipeline_mode=pl.Buffered(1)),
                pl.BlockSpec((tm, D_MODEL), row),
                pl.BlockSpec((1, D_MODEL), const)]
    h_shape = jax.ShapeDtypeStruct((tp, D_MODEL), jnp.float32)
    h_spec = pl.BlockSpec((tm, D_MODEL), row)
    if gnext is None:
        return pl.pallas_call(
            _ffn_out_last_kernel, out_shape=h_shape, grid=(tp // tm,), in_specs=in_specs,
            out_specs=h_spec, compiler_params=_cparams(("parallel",)), name="ffn_out_last",
        )(act, w, h, gpost), None
    return pl.pallas_call(
        _ffn_out_kernel,
        out_shape=(h_shape, jax.ShapeDtypeStruct((tp, D_MODEL), jnp.bfloat16)),
        grid=(tp // tm,),
        in_specs=in_specs + [pl.BlockSpec((1, D_MODEL), const)],
        out_specs=(h_spec, pl.BlockSpec((tm, D_MODEL), row)),
        compiler_params=_cparams(("parallel",)),
        name="ffn_out",
    )(act, w, h, gpost, gnext)


def _prep_weights(w_in, w_uq, w_ukv, w_branch, w_out, w_ffn_in, w_ffn_out):
    bf = jnp.bfloat16
    depth = w_in.shape[0]
    o_kpe = MLA_Q_RANK + MLA_KV_RANK
    o_conv = o_kpe + MLA_ROPE
    o_flog = o_conv + 3 * CONV_WIDTH + 3 * FOX_HEADS * FOX_HEAD_DIM
    o_gate = o_flog + FOX_HEADS
    half = MLA_ROPE // 2

    def rot_cols(w):
        return jnp.concatenate([-w[..., half:], w[..., :half]], axis=-1)

    w_main = jnp.concatenate([w_in[:, :, o_gate:], w_in[:, :, :o_kpe], w_in[:, :, o_conv:o_flog]],
                             axis=-1).astype(bf)
    w_kpe = w_in[:, :, o_kpe:o_conv]
    w_small = jnp.concatenate(
        [w_kpe, rot_cols(w_kpe), w_in[:, :, o_flog:o_gate],
         jnp.zeros((depth, D_MODEL, SMALL_W - 2 * MLA_ROPE - FOX_HEADS), w_in.dtype)], axis=-1).astype(bf)
    wq = w_uq.reshape(depth, MLA_Q_RANK, MLA_HEADS, MLA_NOPE + MLA_ROPE)
    wq_pe = wq[..., MLA_NOPE:]
    wq = jnp.concatenate([wq[..., :MLA_NOPE], wq_pe, rot_cols(wq_pe)], axis=-1)
    wq = wq.reshape(depth, MLA_Q_RANK, MLA_HEADS * 2 * LANES).astype(bf)
    wkv = w_ukv.reshape(depth, MLA_KV_RANK, MLA_HEADS, MLA_NOPE + MLA_V)
    wkv = jnp.concatenate([wkv[..., :MLA_NOPE].reshape(depth, MLA_KV_RANK, -1),
                           wkv[..., MLA_NOPE:].reshape(depth, MLA_KV_RANK, -1)], axis=-1).astype(bf)
    return (w_main, w_small, wq, wkv, w_branch.astype(bf), w_out.astype(bf),
            w_ffn_in.astype(bf), w_ffn_out.astype(bf))


def _rope_tables(lp):
    inv_freq = 1.0 / (ROPE_THETA ** (jnp.arange(0, MLA_ROPE, 2, dtype=jnp.float32) / MLA_ROPE))
    pos = jnp.maximum(jnp.arange(lp, dtype=jnp.float32) - PAD_FRONT, 0.0)
    ang = pos[:, None] * inv_freq[None, :]
    zeros = jnp.zeros((lp, LANES - MLA_ROPE), jnp.float32)
    cos_t = jnp.concatenate([jnp.cos(ang), jnp.cos(ang), zeros], axis=1)
    sin_t = jnp.concatenate([jnp.sin(ang), jnp.sin(ang), zeros], axis=1)
    return cos_t, sin_t


def _decay_selectors():
    src = jnp.arange(3 * LANES)
    dst = jnp.arange(FOX_HEADS * LANES)
    part, head_src = src // LANES, src % LANES
    head_dst, lane_dst = dst // LANES, dst % LANES
    same_head = head_src[:, None] == head_dst[None, :]
    selq = (same_head & (lane_dst[None, :] == part[:, None])).astype(jnp.bfloat16)
    selk = -(same_head & (lane_dst[None, :] == part[:, None] + 3)).astype(jnp.bfloat16)
    return selq, selk


def kernel(x, meta, w_in, b_forget, g_q_lat, g_kv_lat, w_uq, w_ukv, conv_w, w_branch, w_out,
           w_ffn_in, w_ffn_out, g_mix_pre, g_mix_post, g_ffn_pre, g_ffn_post):
    batch, seq, _ = x.shape
    depth = w_in.shape[0]
    lp = PAD_FRONT + N_META + seq
    tp = batch * lp
    assert lp % ATT_BLOCK == 0

    tm_big = lp // 2
    tm_mid = lp // 4
    tm_small = lp // 8

    w_main, w_small, wq, wkv, wb, wo, wfi, wfo = _prep_weights(
        w_in, w_uq, w_ukv, w_branch, w_out, w_ffn_in, w_ffn_out)
    cos_t, sin_t = _rope_tables(lp)
    selq, selk = _decay_selectors()
    tri = (jnp.arange(ATT_BLOCK)[:, None] >= jnp.arange(ATT_BLOCK)[None, :]).astype(jnp.bfloat16)
    bias = jnp.pad(b_forget.astype(jnp.float32), ((0, 0), (0, LANES - FOX_HEADS)))[:, None, :]
    conv_w8 = jnp.pad(conv_w.astype(jnp.float32), ((0, 0), (0, 8 - CONV_K), (0, 0)))
    row2 = lambda g: g.astype(jnp.float32)[:, None, :]
    g_q, g_kv = row2(g_q_lat), row2(g_kv_lat)
    g_mpre, g_mpost, g_fpre, g_fpost = row2(g_mix_pre), row2(g_mix_post), row2(g_ffn_pre), row2(g_ffn_post)

    h = jnp.concatenate([jnp.zeros((batch, PAD_FRONT, D_MODEL), x.dtype),
                         jnp.broadcast_to(meta[None].astype(x.dtype), (batch, N_META, D_MODEL)), x], axis=1)
    h = h.reshape(tp, D_MODEL)
    hn = _prenorm(h, g_mpre[0], tm_mid)

    for l in range(depth):
        proj = _inproj(hn, w_main[l], tm_big)
        small = _smallproj(hn, w_small[l], tm_big)
        qm, qe, km, ke, v_a = _mla_latent(proj, small, g_q[l], g_kv[l], wq[l], wkv[l], cos_t, sin_t, lp, tm_mid)
        o_a = _attention(qm, qe, km, ke, v_a, batch, lp, MLA_HEADS, shared_ke=True)
        aq, ak = _fox_decay(small, bias[l], tri, selq, selk, batch, lp)
        lanes_per_tile = IN_TILE // LANES
        o_c = _attention(proj, aq, proj, ak, proj, batch, lp, FOX_HEADS, shared_ke=False,
                         qm_col=TILE_FQ * lanes_per_tile, km_col=TILE_FK * lanes_per_tile,
                         v_col=TILE_FV * lanes_per_tile)
        o_b = _gated_conv(proj, conv_w8[l], batch, lp, tm_mid)
        h, hn = _merge_out(o_a, o_b, o_c, proj, wb[l], wo[l], h, g_mpost[l], g_fpre[l], tm_small)
        act = _ffn_in(hn, wfi[l], tm_big)
        gnext = g_mpre[l + 1] if l + 1 < depth else None
        h, hn = _ffn_out(act, wfo[l], h, g_fpost[l], gnext, tm_small)

    return h.reshape(batch, lp, D_MODEL)[:, PAD_FRONT + N_META:]
```

```python
import functools

import jax
import jax.numpy as jnp
from jax import lax
from jax.experimental import pallas as pl
from jax.experimental.pallas import tpu as pltpu

D_MODEL = 2048
N_META = 16
EPS = 1e-6
NEG_INF = -1e30
ROPE_THETA = 10000.0
MLA_HEADS = 8
MLA_Q_RANK = 512
MLA_KV_RANK = 512
MLA_NOPE = 128
MLA_ROPE = 64
MLA_V = 128
CONV_WIDTH = 1024
CONV_K = 3
FOX_HEADS = 8
FOX_HEAD_DIM = 128
N_BRANCH = 3
BRANCH_WIDTH = 1024
D_FF = 5632

LOG2E = 1.4426950408889634
MLA_QSCALE = (MLA_NOPE + MLA_ROPE) ** -0.5 * LOG2E
FOX_QSCALE = FOX_HEAD_DIM ** -0.5 * LOG2E

LANES = 128
VMEM_LIMIT_BYTES = 56 * 1024 * 1024

ATT_BLOCK = 128
PAD_FRONT = (-N_META) % ATT_BLOCK

IN_TILE = 1024
N_GATE_TILES = N_BRANCH * D_MODEL // IN_TILE
TILE_LAT = 6
TILE_CONV_B, TILE_CONV_C, TILE_CONV_X = 7, 8, 9
TILE_FQ, TILE_FK, TILE_FV = 10, 11, 12
N_IN_TILES = 13
SMALL_W = 256


def _cparams(semantics):
    return pltpu.CompilerParams(dimension_semantics=semantics, vmem_limit_bytes=VMEM_LIMIT_BYTES)


def _rms(x, g):
    ms = jnp.mean(x * x, axis=-1, keepdims=True)
    return x * lax.rsqrt(ms + EPS) * g


def _prenorm_kernel(h_ref, g_ref, o_ref):
    o_ref[...] = _rms(h_ref[...], g_ref[...]).astype(o_ref.dtype)


def _prenorm(h, g, tm):
    tp = h.shape[0]
    return pl.pallas_call(
        _prenorm_kernel,
        out_shape=jax.ShapeDtypeStruct((tp, D_MODEL), jnp.bfloat16),
        grid=(tp // tm,),
        in_specs=[pl.BlockSpec((tm, D_MODEL), lambda i: (i, 0)),
                  pl.BlockSpec((1, D_MODEL), lambda i: (0, 0))],
        out_specs=pl.BlockSpec((tm, D_MODEL), lambda i: (i, 0)),
        compiler_params=_cparams(("parallel",)),
        name="prenorm",
    )(h, g)


def _inproj_kernel(x_ref, w_ref, o_ref):
    j = pl.program_id(0)
    acc = jnp.dot(x_ref[...], w_ref[...], preferred_element_type=jnp.float32)
    is_gate = j < N_GATE_TILES
    is_fq = j == TILE_FQ

    @pl.when(is_gate)
    def _():
        o_ref[...] = (1.0 / (1.0 + jnp.exp(-acc))).astype(o_ref.dtype)

    @pl.when(is_fq)
    def _():
        o_ref[...] = (acc * FOX_QSCALE).astype(o_ref.dtype)

    @pl.when(jnp.logical_not(jnp.logical_or(is_gate, is_fq)))
    def _():
        o_ref[...] = acc.astype(o_ref.dtype)


def _inproj(xn, w_main, tm):
    tp = xn.shape[0]
    return pl.pallas_call(
        _inproj_kernel,
        out_shape=jax.ShapeDtypeStruct((tp, N_IN_TILES * IN_TILE), jnp.bfloat16),
        grid=(N_IN_TILES, tp // tm),
        in_specs=[pl.BlockSpec((tm, D_MODEL), lambda j, i: (i, 0)),
                  pl.BlockSpec((D_MODEL, IN_TILE), lambda j, i: (0, j))],
        out_specs=pl.BlockSpec((tm, IN_TILE), lambda j, i: (i, j)),
        compiler_params=_cparams(("parallel", "parallel")),
        name="inproj",
    )(xn, w_main)


def _smallproj_kernel(x_ref, w_ref, o_ref):
    o_ref[...] = jnp.dot(x_ref[...], w_ref[...], preferred_element_type=jnp.float32)


def _smallproj(xn, w_small, tm):
    tp = xn.shape[0]
    return pl.pallas_call(
        _smallproj_kernel,
        out_shape=jax.ShapeDtypeStruct((tp, SMALL_W), jnp.float32),
        grid=(tp // tm,),
        in_specs=[pl.BlockSpec((tm, D_MODEL), lambda i: (i, 0)),
                  pl.BlockSpec((D_MODEL, SMALL_W), lambda i: (0, 0))],
        out_specs=pl.BlockSpec((tm, SMALL_W), lambda i: (i, 0)),
        compiler_params=_cparams(("parallel",)),
        name="smallproj",
    )(xn, w_small)


def _rope_pair(v, cos_t, sin_t):
    return v * cos_t + pltpu.roll(v, 64, 1) * sin_t


def _mla_latent_kernel(lat_ref, small_ref, gq_ref, gkv_ref, wq_ref, wkv_ref, cos_ref, sin_ref,
                       qm_ref, qe_ref, km_ref, ke_ref, v_ref):
    lat = lat_ref[...].astype(jnp.float32)
    cos_t = cos_ref[...]
    sin_t = sin_ref[...]
    cq = _rms(lat[:, :MLA_Q_RANK], gq_ref[...]).astype(jnp.bfloat16)
    ckv = _rms(lat[:, MLA_Q_RANK:], gkv_ref[...]).astype(jnp.bfloat16)
    q = jnp.dot(cq, wq_ref[...], preferred_element_type=jnp.float32)
    for h in range(MLA_HEADS):
        base = h * 2 * LANES
        qm_ref[:, h * LANES:(h + 1) * LANES] = (q[:, base:base + LANES] * MLA_QSCALE).astype(qm_ref.dtype)
        roped = _rope_pair(q[:, base + LANES:base + 2 * LANES], cos_t, sin_t)
        qe_ref[:, h * LANES:(h + 1) * LANES] = (roped * MLA_QSCALE).astype(qe_ref.dtype)
    kv = jnp.dot(ckv, wkv_ref[...], preferred_element_type=jnp.float32)
    km_ref[...] = kv[:, :MLA_HEADS * MLA_NOPE].astype(km_ref.dtype)
    v_ref[...] = kv[:, MLA_HEADS * MLA_NOPE:].astype(v_ref.dtype)
    ke_ref[...] = _rope_pair(small_ref[...], cos_t, sin_t).astype(ke_ref.dtype)


def _mla_latent(proj, small, gq, gkv, wq, wkv, cos_t, sin_t, lp, tm):
    tp = proj.shape[0]
    per_batch = lp // tm
    hw = MLA_HEADS * LANES
    bf = jnp.bfloat16
    row = lambda i: (i, 0)
    const = lambda i: (0, 0)
    return pl.pallas_call(
        _mla_latent_kernel,
        out_shape=(jax.ShapeDtypeStruct((tp, hw), bf), jax.ShapeDtypeStruct((tp, hw), bf),
                   jax.ShapeDtypeStruct((tp, hw), bf), jax.ShapeDtypeStruct((tp, LANES), bf),
                   jax.ShapeDtypeStruct((tp, hw), bf)),
        grid=(tp // tm,),
        in_specs=[pl.BlockSpec((tm, IN_TILE), lambda i: (i, TILE_LAT)),
                  pl.BlockSpec((tm, LANES), row),
                  pl.BlockSpec((1, MLA_Q_RANK), const),
                  pl.BlockSpec((1, MLA_KV_RANK), const),
                  pl.BlockSpec((MLA_Q_RANK, 2 * hw), const),
                  pl.BlockSpec((MLA_KV_RANK, 2 * hw), const),
                  pl.BlockSpec((tm, LANES), lambda i: (i % per_batch, 0)),
                  pl.BlockSpec((tm, LANES), lambda i: (i % per_batch, 0))],
        out_specs=(pl.BlockSpec((tm, hw), row), pl.BlockSpec((tm, hw), row),
                   pl.BlockSpec((tm, hw), row), pl.BlockSpec((tm, LANES), row),
                   pl.BlockSpec((tm, hw), row)),
        compiler_params=_cparams(("parallel",)),
        name="mla_latent",
    )(proj, small, gq, gkv, wq, wkv, cos_t, sin_t)


def _split3(x):
    hi = x.astype(jnp.bfloat16)
    r1 = x - hi.astype(jnp.float32)
    mid = r1.astype(jnp.bfloat16)
    lo = (r1 - mid.astype(jnp.float32)).astype(jnp.bfloat16)
    return hi, mid, lo


def _fox_decay_kernel(fl_ref, bias_ref, tri_ref, selq_ref, selk_ref, aq_ref, ak_ref, carry_ref):
    blk = pl.program_id(1)

    @pl.when(blk == 0)
    def _():
        carry_ref[...] = jnp.zeros_like(carry_ref)

    z = fl_ref[...] + bias_ref[...]
    log_f = jnp.minimum(z, 0.0) - jnp.log1p(jnp.exp(-jnp.abs(z)))
    rows = lax.broadcasted_iota(jnp.int32, log_f.shape, 0) + blk * ATT_BLOCK
    log_f = jnp.where(rows >= PAD_FRONT, log_f * LOG2E, 0.0)
    tri = tri_ref[...]
    c = carry_ref[0:1, :]
    for part in _split3(log_f):
        c = c + jnp.dot(tri, part, preferred_element_type=jnp.float32)
    carry_ref[...] = jnp.broadcast_to(c[ATT_BLOCK - 1:ATT_BLOCK, :], carry_ref.shape)
    parts = jnp.concatenate(_split3(c), axis=1)
    lane = lax.broadcasted_iota(jnp.int32, aq_ref.shape, 1) % LANES
    aq = jnp.dot(parts, selq_ref[...], preferred_element_type=jnp.float32)
    ak = jnp.dot(parts, selk_ref[...], preferred_element_type=jnp.float32)
    aq_ref[...] = jnp.where((lane >= 3) & (lane < 6), 1.0, aq).astype(aq_ref.dtype)
    ak_ref[...] = jnp.where(lane < 3, 1.0, ak).astype(ak_ref.dtype)


def _fox_decay(small, bias, tri, selq, selk, batch, lp):
    tp = small.shape[0]
    nblk = lp // ATT_BLOCK
    hw = FOX_HEADS * LANES
    const = lambda b, i: (0, 0)
    row = lambda b, i: (b * nblk + i, 0)
    return pl.pallas_call(
        _fox_decay_kernel,
        out_shape=(jax.ShapeDtypeStruct((tp, hw), jnp.bfloat16),
                   jax.ShapeDtypeStruct((tp, hw), jnp.bfloat16)),
        grid=(batch, nblk),
        in_specs=[pl.BlockSpec((ATT_BLOCK, LANES), lambda b, i: (b * nblk + i, 1)),
                  pl.BlockSpec((1, LANES), const),
                  pl.BlockSpec((ATT_BLOCK, ATT_BLOCK), const),
                  pl.BlockSpec((3 * LANES, hw), const),
                  pl.BlockSpec((3 * LANES, hw), const)],
        out_specs=(pl.BlockSpec((ATT_BLOCK, hw), row), pl.BlockSpec((ATT_BLOCK, hw), row)),
        scratch_shapes=[pltpu.VMEM((8, LANES), jnp.float32)],
        compiler_params=_cparams(("parallel", "arbitrary")),
        name="fox_decay",
    )(small, bias, tri, selq, selk)


Q_TILE = 256


def _attn_tile(qc_ref, kc_ref, v_ref, r0, tq):
    q = qc_ref[r0:r0 + tq, :]

    def scores(k0, k1):
        return lax.dot_general(q, kc_ref[k0:k1, :], (((1,), (1,)), ((), ())),
                               preferred_element_type=jnp.float32)

    col = lax.broadcasted_iota(jnp.int32, (tq, ATT_BLOCK), 1)
    ok = col >= PAD_FRONT
    if r0 == 0:
        ok = ok & (col <= lax.broadcasted_iota(jnp.int32, (tq, ATT_BLOCK), 0))
    pieces = [(jnp.where(ok, scores(0, ATT_BLOCK), NEG_INF), 0, ATT_BLOCK)]
    if r0 > 0:
        if r0 > ATT_BLOCK:
            pieces.append((scores(ATT_BLOCK, r0), ATT_BLOCK, r0))
        causal = (lax.broadcasted_iota(jnp.int32, (tq, tq), 1)
                  <= lax.broadcasted_iota(jnp.int32, (tq, tq), 0))
        pieces.append((jnp.where(causal, scores(r0, r0 + tq), NEG_INF), r0, r0 + tq))
    m = None
    for s, _, _ in pieces:
        pm = jnp.max(s, axis=1, keepdims=True)
        m = pm if m is None else jnp.maximum(m, pm)
    l = None
    acc = None
    for s, k0, k1 in pieces:
        p = jnp.exp2(s - m)
        ps = jnp.sum(p, axis=1, keepdims=True)
        pv = jnp.dot(p.astype(jnp.bfloat16), v_ref[k0:k1, :], preferred_element_type=jnp.float32)
        l = ps if l is None else l + ps
        acc = pv if acc is None else acc + pv
    return acc / l


def _attn_kernel(qm_ref, qe_ref, km_ref, ke_ref, v_ref, o_ref, qc_ref, kc_ref):
    qc_ref[:, :LANES] = qm_ref[...]
    qc_ref[:, LANES:] = qe_ref[...]
    kc_ref[:, :LANES] = km_ref[...]
    kc_ref[:, LANES:] = ke_ref[...]
    lp = o_ref.shape[0]
    tiles = [(0, ATT_BLOCK)] + [(r0, Q_TILE) for r0 in range(ATT_BLOCK, lp, Q_TILE)]
    for r0, tq in tiles:
        o_ref[r0:r0 + tq, :] = _attn_tile(qc_ref, kc_ref, v_ref, r0, tq).astype(o_ref.dtype)


def _attention(qm, qe, km, ke, v, batch, lp, n_heads, shared_ke, qm_col=0, km_col=0, v_col=0):
    assert (lp - ATT_BLOCK) % Q_TILE == 0
    tp = qm.shape[0]
    head = lambda b, h: (b, h)
    at = lambda col: (lambda b, h: (b, col + h))
    ke_map = (lambda b, h: (b, 0)) if shared_ke else head
    blockspec = lambda m: pl.BlockSpec((lp, LANES), m)
    return pl.pallas_call(
        _attn_kernel,
        out_shape=jax.ShapeDtypeStruct((tp, n_heads * LANES), jnp.bfloat16),
        grid=(batch, n_heads),
        in_specs=[blockspec(at(qm_col)), blockspec(head), blockspec(at(km_col)), blockspec(ke_map),
                  blockspec(at(v_col))],
        out_specs=blockspec(head),
        scratch_shapes=[pltpu.VMEM((lp, 2 * LANES), jnp.bfloat16),
                        pltpu.VMEM((lp, 2 * LANES), jnp.bfloat16)],
        compiler_params=_cparams(("parallel", "parallel")),
        name="attention",
    )(qm, qe, km, ke, v)


def _conv_kernel(b_ref, c_ref, x_ref, w_ref, o_ref, tail_ref):
    i = pl.program_id(1)
    tm = o_ref.shape[0]

    @pl.when(i == 0)
    def _():
        tail_ref[...] = jnp.zeros_like(tail_ref)

    u = c_ref[...].astype(jnp.float32) * x_ref[...].astype(jnp.float32)
    rows = lax.broadcasted_iota(jnp.int32, u.shape, 0)
    u = jnp.where(rows + i * tm >= PAD_FRONT, u, 0.0)
    prev1 = tail_ref[7:8, :]
    prev2 = tail_ref[6:7, :]
    u1 = jnp.where(rows == 0, prev1, pltpu.roll(u, 1, 0))
    u2 = jnp.where(rows == 0, prev2, jnp.where(rows == 1, prev1, pltpu.roll(u, 2, 0)))
    w = w_ref[...]
    conv = w[2:3, :] * u + w[1:2, :] * u1 + w[0:1, :] * u2
    o_ref[...] = (b_ref[...].astype(jnp.float32) * conv).astype(o_ref.dtype)
    tail_ref[...] = u[tm - 8:tm, :]


def _gated_conv(proj, conv_w, batch, lp, tm):
    tp = proj.shape[0]
    per_batch = lp // tm
    col = lambda t: (lambda b, i: (b * per_batch + i, t))
    return pl.pallas_call(
        _conv_kernel,
        out_shape=jax.ShapeDtypeStruct((tp, CONV_WIDTH), jnp.bfloat16),
        grid=(batch, per_batch),
        in_specs=[pl.BlockSpec((tm, IN_TILE), col(TILE_CONV_B)),
                  pl.BlockSpec((tm, IN_TILE), col(TILE_CONV_C)),
                  pl.BlockSpec((tm, IN_TILE), col(TILE_CONV_X)),
                  pl.BlockSpec((8, CONV_WIDTH), lambda b, i: (0, 0))],
        out_specs=pl.BlockSpec((tm, CONV_WIDTH), lambda b, i: (b * per_batch + i, 0)),
        scratch_shapes=[pltpu.VMEM((8, CONV_WIDTH), jnp.float32)],
        compiler_params=_cparams(("parallel", "arbitrary")),
        name="gated_conv",
    )(proj, proj, proj, conv_w)


def _residual_epilogue(y, h_ref, gpost_ref, gnext_ref, ho_ref, hn_ref):
    h_new = h_ref[...] + _rms(y, gpost_ref[...])
    ho_ref[...] = h_new
    if hn_ref is not None:
        hn_ref[...] = _rms(h_new, gnext_ref[...]).astype(hn_ref.dtype)


def _merge_out_kernel(oa_ref, ob_ref, oc_ref, gate_ref, wb_ref, wo_ref, h_ref, gpost_ref, gnext_ref,
                      ho_ref, hn_ref):
    merged = None
    for n, o_ref in enumerate((oa_ref, ob_ref, oc_ref)):
        y = jnp.dot(o_ref[...], wb_ref[n], preferred_element_type=jnp.float32)
        term = gate_ref[:, n * D_MODEL:(n + 1) * D_MODEL].astype(jnp.float32) * y
        merged = term if merged is None else merged + term
    mix = jnp.dot(merged.astype(jnp.bfloat16), wo_ref[...], preferred_element_type=jnp.float32)
    _residual_epilogue(mix, h_ref, gpost_ref, gnext_ref, ho_ref, hn_ref)


def _merge_out(o_a, o_b, o_c, proj, wb, wo, h, gpost, gnext, tm):
    tp = h.shape[0]
    row = lambda i: (i, 0)
    const = lambda i: (0, 0)
    single = pl.Buffered(1)
    return pl.pallas_call(
        _merge_out_kernel,
        out_shape=(jax.ShapeDtypeStruct((tp, D_MODEL), jnp.float32),
                   jax.ShapeDtypeStruct((tp, D_MODEL), jnp.bfloat16)),
        grid=(tp // tm,),
        in_specs=[pl.BlockSpec((tm, BRANCH_WIDTH), row),
                  pl.BlockSpec((tm, BRANCH_WIDTH), row),
                  pl.BlockSpec((tm, BRANCH_WIDTH), row),
                  pl.BlockSpec((tm, N_BRANCH * D_MODEL), lambda i: (i, 0)),
                  pl.BlockSpec((N_BRANCH, BRANCH_WIDTH, D_MODEL), lambda i: (0, 0, 0), pipeline_mode=single),
                  pl.BlockSpec((D_MODEL, D_MODEL), const, pipeline_mode=single),
                  pl.BlockSpec((tm, D_MODEL), row),
                  pl.BlockSpec((1, D_MODEL), const),
                  pl.BlockSpec((1, D_MODEL), const)],
        out_specs=(pl.BlockSpec((tm, D_MODEL), row), pl.BlockSpec((tm, D_MODEL), row)),
        compiler_params=_cparams(("parallel",)),
        name="merge_out",
    )(o_a, o_b, o_c, proj, wb, wo, h, gpost, gnext)


FF_TILE = 512


def _ffn_in_kernel(x_ref, wg_ref, wu_ref, o_ref):
    x = x_ref[...]
    g = jnp.dot(x, wg_ref[...], preferred_element_type=jnp.float32)
    u = jnp.dot(x, wu_ref[...], preferred_element_type=jnp.float32)
    o_ref[...] = (g * (1.0 / (1.0 + jnp.exp(-g))) * u).astype(o_ref.dtype)


def _ffn_in(xn, w, tm):
    tp = xn.shape[0]
    n_tiles = D_FF // FF_TILE
    return pl.pallas_call(
        _ffn_in_kernel,
        out_shape=jax.ShapeDtypeStruct((tp, D_FF), jnp.bfloat16),
        grid=(n_tiles, tp // tm),
        in_specs=[pl.BlockSpec((tm, D_MODEL), lambda j, i: (i, 0)),
                  pl.BlockSpec((D_MODEL, FF_TILE), lambda j, i: (0, j)),
                  pl.BlockSpec((D_MODEL, FF_TILE), lambda j, i: (0, j + n_tiles))],
        out_specs=pl.BlockSpec((tm, FF_TILE), lambda j, i: (i, j)),
        compiler_params=_cparams(("parallel", "parallel")),
        name="ffn_in",
    )(xn, w, w)


def _ffn_out_kernel(a_ref, w_ref, h_ref, gpost_ref, gnext_ref, ho_ref, hn_ref):
    y = jnp.dot(a_ref[...], w_ref[...], preferred_element_type=jnp.float32)
    _residual_epilogue(y, h_ref, gpost_ref, gnext_ref, ho_ref, hn_ref)


def _ffn_out_last_kernel(a_ref, w_ref, h_ref, gpost_ref, ho_ref):
    y = jnp.dot(a_ref[...], w_ref[...], preferred_element_type=jnp.float32)
    _residual_epilogue(y, h_ref, gpost_ref, None, ho_ref, None)


def _ffn_out(act, w, h, gpost, gnext, tm):
    tp = h.shape[0]
    row = lambda i: (i, 0)
    const = lambda i: (0, 0)
    in_specs = [pl.BlockSpec((tm, D_FF), row),
                pl.BlockSpec((D_FF, D_MODEL), const, pipeline_mode=pl.Buffered(1)),
                pl.BlockSpec((tm, D_MODEL), row),
                pl.BlockSpec((1, D_MODEL), const)]
    h_shape = jax.ShapeDtypeStruct((tp, D_MODEL), jnp.float32)
    h_spec = pl.BlockSpec((tm, D_MODEL), row)
    if gnext is None:
        return pl.pallas_call(
            _ffn_out_last_kernel, out_shape=h_shape, grid=(tp // tm,), in_specs=in_specs,
            out_specs=h_spec, compiler_params=_cparams(("parallel",)), name="ffn_out_last",
        )(act, w, h, gpost), None
    return pl.pallas_call(
        _ffn_out_kernel,
        out_shape=(h_shape, jax.ShapeDtypeStruct((tp, D_MODEL), jnp.bfloat16)),
        grid=(tp // tm,),
        in_specs=in_specs + [pl.BlockSpec((1, D_MODEL), const)],
        out_specs=(h_spec, pl.BlockSpec((tm, D_MODEL), row)),
        compiler_params=_cparams(("parallel",)),
        name="ffn_out",
    )(act, w, h, gpost, gnext)


def _prep_weights(w_in, w_uq, w_ukv, w_branch, w_out, w_ffn_in, w_ffn_out):
    bf = jnp.bfloat16
    depth = w_in.shape[0]
    o_kpe = MLA_Q_RANK + MLA_KV_RANK
    o_conv = o_kpe + MLA_ROPE
    o_flog = o_conv + 3 * CONV_WIDTH + 3 * FOX_HEADS * FOX_HEAD_DIM
    o_gate = o_flog + FOX_HEADS
    half = MLA_ROPE // 2

    def rot_cols(w):
        return jnp.concatenate([-w[..., half:], w[..., :half]], axis=-1)

    w_main = jnp.concatenate([w_in[:, :, o_gate:], w_in[:, :, :o_kpe], w_in[:, :, o_conv:o_flog]],
                             axis=-1).astype(bf)
    w_kpe = w_in[:, :, o_kpe:o_conv]
    w_small = jnp.concatenate(
        [w_kpe, rot_cols(w_kpe), w_in[:, :, o_flog:o_gate],
         jnp.zeros((depth, D_MODEL, SMALL_W - 2 * MLA_ROPE - FOX_HEADS), w_in.dtype)], axis=-1).astype(bf)
    wq = w_uq.reshape(depth, MLA_Q_RANK, MLA_HEADS, MLA_NOPE + MLA_ROPE)
    wq_pe = wq[..., MLA_NOPE:]
    wq = jnp.concatenate([wq[..., :MLA_NOPE], wq_pe, rot_cols(wq_pe)], axis=-1)
    wq = wq.reshape(depth, MLA_Q_RANK, MLA_HEADS * 2 * LANES).astype(bf)
    wkv = w_ukv.reshape(depth, MLA_KV_RANK, MLA_HEADS, MLA_NOPE + MLA_V)
    wkv = jnp.concatenate([wkv[..., :MLA_NOPE].reshape(depth, MLA_KV_RANK, -1),
                           wkv[..., MLA_NOPE:].reshape(depth, MLA_KV_RANK, -1)], axis=-1).astype(bf)
    return (w_main, w_small, wq, wkv, w_branch.astype(bf), w_out.astype(bf),
            w_ffn_in.astype(bf), w_ffn_out.astype(bf))


def _rope_tables(lp):
    inv_freq = 1.0 / (ROPE_THETA ** (jnp.arange(0, MLA_ROPE, 2, dtype=jnp.float32) / MLA_ROPE))
    pos = jnp.maximum(jnp.arange(lp, dtype=jnp.float32) - PAD_FRONT, 0.0)
    ang = pos[:, None] * inv_freq[None, :]
    zeros = jnp.zeros((lp, LANES - MLA_ROPE), jnp.float32)
    cos_t = jnp.concatenate([jnp.cos(ang), jnp.cos(ang), zeros], axis=1)
    sin_t = jnp.concatenate([jnp.sin(ang), jnp.sin(ang), zeros], axis=1)
    return cos_t, sin_t


def _decay_selectors():
    src = jnp.arange(3 * LANES)
    dst = jnp.arange(FOX_HEADS * LANES)
    part, head_src = src // LANES, src % LANES
    head_dst, lane_dst = dst // LANES, dst % LANES
    same_head = head_src[:, None] == head_dst[None, :]
    selq = (same_head & (lane_dst[None, :] == part[:, None])).astype(jnp.bfloat16)
    selk = -(same_head & (lane_dst[None, :] == part[:, None] + 3)).astype(jnp.bfloat16)
    return selq, selk


def kernel(x, meta, w_in, b_forget, g_q_lat, g_kv_lat, w_uq, w_ukv, conv_w, w_branch, w_out,
           w_ffn_in, w_ffn_out, g_mix_pre, g_mix_post, g_ffn_pre, g_ffn_post):
    batch, seq, _ = x.shape
    depth = w_in.shape[0]
    lp = PAD_FRONT + N_META + seq
    tp = batch * lp
    assert lp % ATT_BLOCK == 0

    tm_big = lp // 2
    tm_mid = lp // 4
    tm_small = lp // 8

    w_main, w_small, wq, wkv, wb, wo, wfi, wfo = _prep_weights(
        w_in, w_uq, w_ukv, w_branch, w_out, w_ffn_in, w_ffn_out)
    cos_t, sin_t = _rope_tables(lp)
    selq, selk = _decay_selectors()
    tri = (jnp.arange(ATT_BLOCK)[:, None] >= jnp.arange(ATT_BLOCK)[None, :]).astype(jnp.bfloat16)
    bias = jnp.pad(b_forget.astype(jnp.float32), ((0, 0), (0, LANES - FOX_HEADS)))[:, None, :]
    conv_w8 = jnp.pad(conv_w.astype(jnp.float32), ((0, 0), (0, 8 - CONV_K), (0, 0)))
    row2 = lambda g: g.astype(jnp.float32)[:, None, :]
    g_q, g_kv = row2(g_q_lat), row2(g_kv_lat)
    g_mpre, g_mpost, g_fpre, g_fpost = row2(g_mix_pre), row2(g_mix_post), row2(g_ffn_pre), row2(g_ffn_post)

    h = jnp.concatenate([jnp.zeros((batch, PAD_FRONT, D_MODEL), x.dtype),
                         jnp.broadcast_to(meta[None].astype(x.dtype), (batch, N_META, D_MODEL)), x], axis=1)
    h = h.reshape(tp, D_MODEL)
    hn = _prenorm(h, g_mpre[0], tm_mid)

    for l in range(depth):
        proj = _inproj(hn, w_main[l], tm_big)
        small = _smallproj(hn, w_small[l], tm_big)
        qm, qe, km, ke, v_a = _mla_latent(proj, small, g_q[l], g_kv[l], wq[l], wkv[l], cos_t, sin_t, lp, tm_mid)
        o_a = _attention(qm, qe, km, ke, v_a, batch, lp, MLA_HEADS, shared_ke=True)
        aq, ak = _fox_decay(small, bias[l], tri, selq, selk, batch, lp)
        lanes_per_tile = IN_TILE // LANES
        o_c = _attention(proj, aq, proj, ak, proj, batch, lp, FOX_HEADS, shared_ke=False,
                         qm_col=TILE_FQ * lanes_per_tile, km_col=TILE_FK * lanes_per_tile,
                         v_col=TILE_FV * lanes_per_tile)
        o_b = _gated_conv(proj, conv_w8[l], batch, lp, tm_mid)
        h, hn = _merge_out(o_a, o_b, o_c, proj, wb[l], wo[l], h, g_mpost[l], g_fpre[l], tm_small)
        act = _ffn_in(hn, wfi[l], tm_big)
        gnext = g_mpre[l + 1] if l + 1 < depth else None
        h, hn = _ffn_out(act, wfo[l], h, g_fpost[l], gnext, tm_small)

    return h.reshape(batch, lp, D_MODEL)[:, PAD_FRONT + N_META:]
```

```python
import functools

import jax
import jax.numpy as jnp
from jax import lax
from jax.experimental import pallas as pl
from jax.experimental.pallas import tpu as pltpu

D_MODEL = 2048
N_META = 16
EPS = 1e-6
NEG_INF = -1e30
ROPE_THETA = 10000.0
MLA_HEADS = 8
MLA_Q_RANK = 512
MLA_KV_RANK = 512
MLA_NOPE = 128
MLA_ROPE = 64
MLA_V = 128
CONV_WIDTH = 1024
CONV_K = 3
FOX_HEADS = 8
FOX_HEAD_DIM = 128
N_BRANCH = 3
BRANCH_WIDTH = 1024
D_FF = 5632

LOG2E = 1.4426950408889634
MLA_QSCALE = (MLA_NOPE + MLA_ROPE) ** -0.5 * LOG2E
FOX_QSCALE = FOX_HEAD_DIM ** -0.5 * LOG2E

LANES = 128
MXU_N = 256
VMEM_LIMIT_BYTES = 56 * 1024 * 1024

ATT_BLOCK = 128
PAD_FRONT = (-N_META) % ATT_BLOCK

IN_TILE = 1024
COL_KPE = MLA_Q_RANK + MLA_KV_RANK
COL_CONV = COL_KPE + MLA_ROPE
N_CONVF_TILES = (3 * CONV_WIDTH + 3 * FOX_HEADS * FOX_HEAD_DIM) // IN_TILE
COL_FLOG = COL_CONV + N_CONVF_TILES * IN_TILE
COL_GATE = COL_FLOG + FOX_HEADS
N_GATE_TILES = N_BRANCH * D_MODEL // IN_TILE
FLOG_LANE = COL_FLOG % LANES
CAST_ROWS = 256
FF_TILE = 512


def _cparams(semantics):
    return pltpu.CompilerParams(dimension_semantics=semantics, vmem_limit_bytes=VMEM_LIMIT_BYTES)


def _rms(x, g):
    ms = jnp.mean(x * x, axis=-1, keepdims=True)
    return x * lax.rsqrt(ms + EPS) * g


def _sigmoid(x):
    return 1.0 / (1.0 + jnp.exp(-x))


def _layer_spec(shape, l):
    zeros = (0,) * len(shape)
    return pl.BlockSpec((None,) + tuple(shape), lambda *_: (l,) + zeros)


def _prenorm_kernel(h_ref, g_ref, o_ref):
    o_ref[...] = _rms(h_ref[...], g_ref[...]).astype(o_ref.dtype)


def _prenorm(h, g, l, tm):
    tp = h.shape[0]
    return pl.pallas_call(
        _prenorm_kernel,
        out_shape=jax.ShapeDtypeStruct((tp, D_MODEL), jnp.bfloat16),
        grid=(tp // tm,),
        in_specs=[pl.BlockSpec((tm, D_MODEL), lambda i: (i, 0)), _layer_spec((1, D_MODEL), l)],
        out_specs=pl.BlockSpec((tm, D_MODEL), lambda i: (i, 0)),
        compiler_params=_cparams(("parallel",)),
        name="prenorm",
    )(h, g)


def _wproj_kernel(x_ref, wm_ref, we_ref, srow_ref, o_ref, w_scr, *, shift, sigmoid):
    @pl.when(pl.program_id(1) == 0)
    def _():
        for r in range(0, D_MODEL, CAST_ROWS):
            w = wm_ref[r:r + CAST_ROWS, :]
            if shift:
                w = jnp.concatenate([w, we_ref[r:r + CAST_ROWS, :]], axis=1)[:, shift:shift + IN_TILE]
            w_scr[r:r + CAST_ROWS, :] = w.astype(jnp.bfloat16)

    x = x_ref[...]
    for c in range(0, IN_TILE, MXU_N):
        acc = jnp.dot(x, w_scr[:, c:c + MXU_N], preferred_element_type=jnp.float32)
        y = _sigmoid(acc) if sigmoid else acc * srow_ref[:, c:c + MXU_N]
        o_ref[:, c:c + MXU_N] = y.astype(o_ref.dtype)


def _wproj(xn, w_in, srow, l, col0, n_tiles, sigmoid, tm):
    tp = xn.shape[0]
    shift = col0 % LANES
    tile0 = col0 // IN_TILE
    assert (col0 - shift) % IN_TILE == 0
    ext_per_tile = IN_TILE // LANES
    return pl.pallas_call(
        functools.partial(_wproj_kernel, shift=shift, sigmoid=sigmoid),
        out_shape=jax.ShapeDtypeStruct((tp, n_tiles * IN_TILE), jnp.bfloat16),
        grid=(n_tiles, tp // tm),
        in_specs=[pl.BlockSpec((tm, D_MODEL), lambda j, i: (i, 0)),
                  pl.BlockSpec((None, D_MODEL, IN_TILE), lambda j, i: (l, 0, tile0 + j)),
                  pl.BlockSpec((None, D_MODEL, LANES), lambda j, i: (l, 0, (tile0 + j + 1) * ext_per_tile)),
                  pl.BlockSpec((1, IN_TILE), lambda j, i: (0, j))],
        out_specs=pl.BlockSpec((tm, IN_TILE), lambda j, i: (i, j)),
        scratch_shapes=[pltpu.VMEM((D_MODEL, IN_TILE), jnp.bfloat16)],
        compiler_params=_cparams(("arbitrary", "arbitrary")),
        name="wproj_gates" if sigmoid else "wproj_convf",
    )(xn, w_in, w_in, srow)


def _latproj_kernel(x_ref, wl_ref, wk_ref, wf_ref, lat_ref, small_ref, wl_scr, ws_scr):
    @pl.when(pl.program_id(0) == 0)
    def _():
        for r in range(0, D_MODEL, CAST_ROWS):
            wl_scr[r:r + CAST_ROWS, :] = wl_ref[r:r + CAST_ROWS, :].astype(jnp.bfloat16)
        v = wk_ref[...]
        lane = lax.broadcasted_iota(jnp.int32, v.shape, 1)
        half = MLA_ROPE // 2
        rot = jnp.where(lane < MLA_ROPE + half, -pltpu.roll(v, half, 1), pltpu.roll(v, LANES - half, 1))
        ws_scr[:, :LANES] = jnp.where(lane < MLA_ROPE, v, rot).astype(jnp.bfloat16)
        ws_scr[:, LANES:] = wf_ref[...].astype(jnp.bfloat16)

    x = x_ref[...]
    for c in range(0, IN_TILE, MXU_N):
        lat_ref[:, c:c + MXU_N] = jnp.dot(x, wl_scr[:, c:c + MXU_N],
                                          preferred_element_type=jnp.float32).astype(lat_ref.dtype)
    small_ref[...] = jnp.dot(x, ws_scr[...], preferred_element_type=jnp.float32)


def _latproj(xn, w_in, l, tm):
    tp = xn.shape[0]
    single = pl.Buffered(1)
    return pl.pallas_call(
        _latproj_kernel,
        out_shape=(jax.ShapeDtypeStruct((tp, IN_TILE), jnp.bfloat16),
                   jax.ShapeDtypeStruct((tp, 2 * LANES), jnp.float32)),
        grid=(tp // tm,),
        in_specs=[pl.BlockSpec((tm, D_MODEL), lambda i: (i, 0)),
                  pl.BlockSpec((None, D_MODEL, IN_TILE), lambda i: (l, 0, 0), pipeline_mode=single),
                  pl.BlockSpec((None, D_MODEL, LANES), lambda i: (l, 0, COL_KPE // LANES)),
                  pl.BlockSpec((None, D_MODEL, LANES), lambda i: (l, 0, COL_FLOG // LANES))],
        out_specs=(pl.BlockSpec((tm, IN_TILE), lambda i: (i, 0)),
                   pl.BlockSpec((tm, 2 * LANES), lambda i: (i, 0))),
        scratch_shapes=[pltpu.VMEM((D_MODEL, IN_TILE), jnp.bfloat16),
                        pltpu.VMEM((D_MODEL, 2 * LANES), jnp.bfloat16)],
        compiler_params=_cparams(("arbitrary",)),
        name="latproj",
    )(xn, w_in, w_in, w_in)


def _rope_pair(v, cos_t, sin_t):
    return v * cos_t + pltpu.roll(v, 64, 1) * sin_t


def _mla_latent_kernel(lat_ref, small_ref, gq_ref, gkv_ref, wq_ref, wkv_ref, cos_ref, sin_ref,
                       qm_ref, qe_ref, km_ref, ke_ref, v_ref):
    lat = lat_ref[...].astype(jnp.float32)
    cos_t = cos_ref[...]
    sin_t = sin_ref[...]
    cq = _rms(lat[:, :MLA_Q_RANK], gq_ref[...]).astype(jnp.bfloat16)
    ckv = _rms(lat[:, MLA_Q_RANK:], gkv_ref[...]).astype(jnp.bfloat16)
    q = jnp.dot(cq, wq_ref[...], preferred_element_type=jnp.float32)
    for h in range(MLA_HEADS):
        base = h * 2 * LANES
        qm_ref[:, h * LANES:(h + 1) * LANES] = (q[:, base:base + LANES] * MLA_QSCALE).astype(qm_ref.dtype)
        roped = _rope_pair(q[:, base + LANES:base + 2 * LANES], cos_t, sin_t)
        qe_ref[:, h * LANES:(h + 1) * LANES] = (roped * MLA_QSCALE).astype(qe_ref.dtype)
    kv = jnp.dot(ckv, wkv_ref[...], preferred_element_type=jnp.float32)
    km_ref[...] = kv[:, :MLA_HEADS * MLA_NOPE].astype(km_ref.dtype)
    v_ref[...] = kv[:, MLA_HEADS * MLA_NOPE:].astype(v_ref.dtype)
    ke_ref[...] = _rope_pair(small_ref[...], cos_t, sin_t).astype(ke_ref.dtype)


def _mla_latent(lat, small, gq, gkv, wq, wkv, cos_t, sin_t, l, lp, tm):
    tp = lat.shape[0]
    per_batch = lp // tm
    hw = MLA_HEADS * LANES
    bf = jnp.bfloat16
    row = lambda i: (i, 0)
    return pl.pallas_call(
        _mla_latent_kernel,
        out_shape=(jax.ShapeDtypeStruct((tp, hw), bf), jax.ShapeDtypeStruct((tp, hw), bf),
                   jax.ShapeDtypeStruct((tp, hw), bf), jax.ShapeDtypeStruct((tp, LANES), bf),
                   jax.ShapeDtypeStruct((tp, hw), bf)),
        grid=(tp // tm,),
        in_specs=[pl.BlockSpec((tm, IN_TILE), row),
                  pl.BlockSpec((tm, LANES), row),
                  _layer_spec((1, MLA_Q_RANK), l),
                  _layer_spec((1, MLA_KV_RANK), l),
                  _layer_spec((MLA_Q_RANK, 2 * hw), l),
                  _layer_spec((MLA_KV_RANK, 2 * hw), l),
                  pl.BlockSpec((tm, LANES), lambda i: (i % per_batch, 0)),
                  pl.BlockSpec((tm, LANES), lambda i: (i % per_batch, 0))],
        out_specs=(pl.BlockSpec((tm, hw), row), pl.BlockSpec((tm, hw), row),
                   pl.BlockSpec((tm, hw), row), pl.BlockSpec((tm, LANES), row),
                   pl.BlockSpec((tm, hw), row)),
        compiler_params=_cparams(("parallel",)),
        name="mla_latent",
    )(lat, small, gq, gkv, wq, wkv, cos_t, sin_t)


def _split3(x):
    hi = x.astype(jnp.bfloat16)
    r1 = x - hi.astype(jnp.float32)
    mid = r1.astype(jnp.bfloat16)
    lo = (r1 - mid.astype(jnp.float32)).astype(jnp.bfloat16)
    return hi, mid, lo


def _fox_decay_kernel(fl_ref, bias_ref, tri_ref, selq_ref, selk_ref, aq_ref, ak_ref, carry_ref):
    blk = pl.program_id(1)

    @pl.when(blk == 0)
    def _():
        carry_ref[...] = jnp.zeros_like(carry_ref)

    z = fl_ref[...] + bias_ref[...]
    log_f = jnp.minimum(z, 0.0) - jnp.log1p(jnp.exp(-jnp.abs(z)))
    rows = lax.broadcasted_iota(jnp.int32, log_f.shape, 0) + blk * ATT_BLOCK
    log_f = jnp.where(rows >= PAD_FRONT, log_f * LOG2E, 0.0)
    tri = tri_ref[...]
    c = carry_ref[0:1, :]
    for part in _split3(log_f):
        c = c + jnp.dot(tri, part, preferred_element_type=jnp.float32)
    carry_ref[...] = jnp.broadcast_to(c[ATT_BLOCK - 1:ATT_BLOCK, :], carry_ref.shape)
    parts = jnp.concatenate(_split3(c), axis=1)
    lane = lax.broadcasted_iota(jnp.int32, aq_ref.shape, 1) % LANES
    aq = jnp.dot(parts, selq_ref[...], preferred_element_type=jnp.float32)
    ak = jnp.dot(parts, selk_ref[...], preferred_element_type=jnp.float32)
    aq_ref[...] = jnp.where((lane >= 3) & (lane < 6), 1.0, aq).astype(aq_ref.dtype)
    ak_ref[...] = jnp.where(lane < 3, 1.0, ak).astype(ak_ref.dtype)


def _fox_decay(small, bias, tri, selq, selk, l, batch, lp):
    tp = small.shape[0]
    nblk = lp // ATT_BLOCK
    hw = FOX_HEADS * LANES
    const = lambda b, i: (0, 0)
    row = lambda b, i: (b * nblk + i, 0)
    return pl.pallas_call(
        _fox_decay_kernel,
        out_shape=(jax.ShapeDtypeStruct((tp, hw), jnp.bfloat16),
                   jax.ShapeDtypeStruct((tp, hw), jnp.bfloat16)),
        grid=(batch, nblk),
        in_specs=[pl.BlockSpec((ATT_BLOCK, LANES), lambda b, i: (b * nblk + i, 1)),
                  _layer_spec((1, LANES), l),
                  pl.BlockSpec((ATT_BLOCK, ATT_BLOCK), const),
                  pl.BlockSpec((3 * LANES, hw), const),
                  pl.BlockSpec((3 * LANES, hw), const)],
        out_specs=(pl.BlockSpec((ATT_BLOCK, hw), row), pl.BlockSpec((ATT_BLOCK, hw), row)),
        scratch_shapes=[pltpu.VMEM((8, LANES), jnp.float32)],
        compiler_params=_cparams(("parallel", "arbitrary")),
        name="fox_decay",
    )(small, bias, tri, selq, selk)


Q_TILE = 256


def _attn_tile(qc_ref, kc_ref, v_ref, r0, tq):
    q = qc_ref[r0:r0 + tq, :]

    def scores(k0, k1):
        return lax.dot_general(q, kc_ref[k0:k1, :], (((1,), (1,)), ((), ())),
                               preferred_element_type=jnp.float32)

    col = lax.broadcasted_iota(jnp.int32, (tq, ATT_BLOCK), 1)
    ok = col >= PAD_FRONT
    if r0 == 0:
        ok = ok & (col <= lax.broadcasted_iota(jnp.int32, (tq, ATT_BLOCK), 0))
    pieces = [(jnp.where(ok, scores(0, ATT_BLOCK), NEG_INF), 0, ATT_BLOCK)]
    if r0 > 0:
        if r0 > ATT_BLOCK:
            pieces.append((scores(ATT_BLOCK, r0), ATT_BLOCK, r0))
        causal = (lax.broadcasted_iota(jnp.int32, (tq, tq), 1)
                  <= lax.broadcasted_iota(jnp.int32, (tq, tq), 0))
        pieces.append((jnp.where(causal, scores(r0, r0 + tq), NEG_INF), r0, r0 + tq))
    m = None
    for s, _, _ in pieces:
        pm = jnp.max(s, axis=1, keepdims=True)
        m = pm if m is None else jnp.maximum(m, pm)
    l = None
    acc = None
    for s, k0, k1 in pieces:
        p = jnp.exp2(s - m)
        ps = jnp.sum(p, axis=1, keepdims=True)
        pv = jnp.dot(p.astype(jnp.bfloat16), v_ref[k0:k1, :], preferred_element_type=jnp.float32)
        l = ps if l is None else l + ps
        acc = pv if acc is None else acc + pv
    return acc / l


def _attn_kernel(qm_ref, qe_ref, km_ref, ke_ref, v_ref, o_ref, qc_ref, kc_ref):
    qc_ref[:, :LANES] = qm_ref[...]
    qc_ref[:, LANES:] = qe_ref[...]
    kc_ref[:, :LANES] = km_ref[...]
    kc_ref[:, LANES:] = ke_ref[...]
    lp = o_ref.shape[0]
    tiles = [(0, ATT_BLOCK)] + [(r0, Q_TILE) for r0 in range(ATT_BLOCK, lp, Q_TILE)]
    for r0, tq in tiles:
        o_ref[r0:r0 + tq, :] = _attn_tile(qc_ref, kc_ref, v_ref, r0, tq).astype(o_ref.dtype)


def _attention(qm, qe, km, ke, v, batch, lp, n_heads, shared_ke, qm_col=0, km_col=0, v_col=0):
    assert (lp - ATT_BLOCK) % Q_TILE == 0
    tp = qm.shape[0]
    head = lambda b, h: (b, h)
    at = lambda col: (lambda b, h: (b, col + h))
    ke_map = (lambda b, h: (b, 0)) if shared_ke else head
    blockspec = lambda m: pl.BlockSpec((lp, LANES), m)
    return pl.pallas_call(
        _attn_kernel,
        out_shape=jax.ShapeDtypeStruct((tp, n_heads * LANES), jnp.bfloat16),
        grid=(batch, n_heads),
        in_specs=[blockspec(at(qm_col)), blockspec(head), blockspec(at(km_col)), blockspec(ke_map),
                  blockspec(at(v_col))],
        out_specs=blockspec(head),
        scratch_shapes=[pltpu.VMEM((lp, 2 * LANES), jnp.bfloat16),
                        pltpu.VMEM((lp, 2 * LANES), jnp.bfloat16)],
        compiler_params=_cparams(("parallel", "parallel")),
        name="attention",
    )(qm, qe, km, ke, v)


def _conv_kernel(b_ref, c_ref, x_ref, w_ref, o_ref, tail_ref):
    i = pl.program_id(1)
    tm = o_ref.shape[0]

    @pl.when(i == 0)
    def _():
        tail_ref[...] = jnp.zeros_like(tail_ref)

    u = c_ref[...].astype(jnp.float32) * x_ref[...].astype(jnp.float32)
    rows = lax.broadcasted_iota(jnp.int32, u.shape, 0)
    u = jnp.where(rows + i * tm >= PAD_FRONT, u, 0.0)
    prev1 = tail_ref[7:8, :]
    prev2 = tail_ref[6:7, :]
    u1 = jnp.where(rows == 0, prev1, pltpu.roll(u, 1, 0))
    u2 = jnp.where(rows == 0, prev2, jnp.where(rows == 1, prev1, pltpu.roll(u, 2, 0)))
    w = w_ref[...]
    conv = w[2:3, :] * u + w[1:2, :] * u1 + w[0:1, :] * u2
    o_ref[...] = (b_ref[...].astype(jnp.float32) * conv).astype(o_ref.dtype)
    tail_ref[...] = u[tm - 8:tm, :]


def _gated_conv(convf, conv_w, l, batch, lp, tm):
    tp = convf.shape[0]
    per_batch = lp // tm
    col = lambda t: (lambda b, i: (b * per_batch + i, t))
    return pl.pallas_call(
        _conv_kernel,
        out_shape=jax.ShapeDtypeStruct((tp, CONV_WIDTH), jnp.bfloat16),
        grid=(batch, per_batch),
        in_specs=[pl.BlockSpec((tm, IN_TILE), col(0)),
                  pl.BlockSpec((tm, IN_TILE), col(1)),
                  pl.BlockSpec((tm, IN_TILE), col(2)),
                  _layer_spec((8, CONV_WIDTH), l)],
        out_specs=pl.BlockSpec((tm, CONV_WIDTH), lambda b, i: (b * per_batch + i, 0)),
        scratch_shapes=[pltpu.VMEM((8, CONV_WIDTH), jnp.float32)],
        compiler_params=_cparams(("parallel", "arbitrary")),
        name="gated_conv",
    )(convf, convf, convf, conv_w)


def _residual_epilogue(y, h_ref, gpost_ref, gnext_ref, ho_ref, hn_ref):
    h_new = h_ref[...] + _rms(y, gpost_ref[...])
    ho_ref[...] = h_new
    if hn_ref is not None:
        hn_ref[...] = _rms(h_new, gnext_ref[...]).astype(hn_ref.dtype)


def _merge_out_kernel(oa_ref, ob_ref, oc_ref, gate_ref, wb_ref, wo_ref, h_ref, gpost_ref, gnext_ref,
                      ho_ref, hn_ref):
    merged = None
    for n, o_ref in enumerate((oa_ref, ob_ref, oc_ref)):
        y = jnp.dot(o_ref[...], wb_ref[n], preferred_element_type=jnp.float32)
        term = gate_ref[:, n * D_MODEL:(n + 1) * D_MODEL].astype(jnp.float32) * y
        merged = term if merged is None else merged + term
    mix = jnp.dot(merged.astype(jnp.bfloat16), wo_ref[...], preferred_element_type=jnp.float32)
    _residual_epilogue(mix, h_ref, gpost_ref, gnext_ref, ho_ref, hn_ref)


def _merge_out(o_a, o_b, o_c, gates, wb, wo, h, gpost, gnext, l, tm):
    tp = h.shape[0]
    row = lambda i: (i, 0)
    single = pl.Buffered(1)
    return pl.pallas_call(
        _merge_out_kernel,
        out_shape=(jax.ShapeDtypeStruct((tp, D_MODEL), jnp.float32),
                   jax.ShapeDtypeStruct((tp, D_MODEL), jnp.bfloat16)),
        grid=(tp // tm,),
        in_specs=[pl.BlockSpec((tm, BRANCH_WIDTH), row),
                  pl.BlockSpec((tm, BRANCH_WIDTH), row),
                  pl.BlockSpec((tm, BRANCH_WIDTH), row),
                  pl.BlockSpec((tm, N_BRANCH * D_MODEL), row),
                  pl.BlockSpec((None, N_BRANCH, BRANCH_WIDTH, D_MODEL), lambda i: (l, 0, 0, 0),
                               pipeline_mode=single),
                  pl.BlockSpec((None, D_MODEL, D_MODEL), lambda i: (l, 0, 0), pipeline_mode=single),
                  pl.BlockSpec((tm, D_MODEL), row),
                  _layer_spec((1, D_MODEL), l),
                  _layer_spec((1, D_MODEL), l)],
        out_specs=(pl.BlockSpec((tm, D_MODEL), row), pl.BlockSpec((tm, D_MODEL), row)),
        compiler_params=_cparams(("parallel",)),
        name="merge_out",
    )(o_a, o_b, o_c, gates, wb, wo, h, gpost, gnext)


def _ffn_in_kernel(x_ref, wg_ref, wu_ref, o_ref, wg_scr, wu_scr):
    @pl.when(pl.program_id(1) == 0)
    def _():
        for r in range(0, D_MODEL, CAST_ROWS):
            wg_scr[r:r + CAST_ROWS, :] = wg_ref[r:r + CAST_ROWS, :].astype(jnp.bfloat16)
            wu_scr[r:r + CAST_ROWS, :] = wu_ref[r:r + CAST_ROWS, :].astype(jnp.bfloat16)

    x = x_ref[...]
    for c in range(0, FF_TILE, MXU_N):
        g = jnp.dot(x, wg_scr[:, c:c + MXU_N], preferred_element_type=jnp.float32)
        u = jnp.dot(x, wu_scr[:, c:c + MXU_N], preferred_element_type=jnp.float32)
        o_ref[:, c:c + MXU_N] = (g * _sigmoid(g) * u).astype(o_ref.dtype)


def _ffn_in(xn, w, l, tm):
    tp = xn.shape[0]
    n_tiles = D_FF // FF_TILE
    return pl.pallas_call(
        _ffn_in_kernel,
        out_shape=jax.ShapeDtypeStruct((tp, D_FF), jnp.bfloat16),
        grid=(n_tiles, tp // tm),
        in_specs=[pl.BlockSpec((tm, D_MODEL), lambda j, i: (i, 0)),
                  pl.BlockSpec((None, D_MODEL, FF_TILE), lambda j, i: (l, 0, j)),
                  pl.BlockSpec((None, D_MODEL, FF_TILE), lambda j, i: (l, 0, j + n_tiles))],
        out_specs=pl.BlockSpec((tm, FF_TILE), lambda j, i: (i, j)),
        scratch_shapes=[pltpu.VMEM((D_MODEL, FF_TILE), jnp.bfloat16),
                        pltpu.VMEM((D_MODEL, FF_TILE), jnp.bfloat16)],
        compiler_params=_cparams(("arbitrary", "arbitrary")),
        name="ffn_in",
    )(xn, w, w)


def _ffn_out_kernel(a_ref, w_ref, h_ref, gpost_ref, gnext_ref, ho_ref, hn_ref):
    y = jnp.dot(a_ref[...], w_ref[...], preferred_element_type=jnp.float32)
    _residual_epilogue(y, h_ref, gpost_ref, gnext_ref, ho_ref, hn_ref)


def _ffn_out_last_kernel(a_ref, w_ref, h_ref, gpost_ref, ho_ref):
    y = jnp.dot(a_ref[...], w_ref[...], preferred_element_type=jnp.float32)
    _residual_epilogue(y, h_ref, gpost_ref, None, ho_ref, None)


def _ffn_out(act, w, h, gpost, gnext, l, tm):
    tp = h.shape[0]
    row = lambda i: (i, 0)
    in_specs = [pl.BlockSpec((tm, D_FF), row),
                pl.BlockSpec((None, D_FF, D_MODEL), lambda i: (l, 0, 0), pipeline_mode=pl.Buffered(1)),
                pl.BlockSpec((tm, D_MODEL), row),
                _layer_spec((1, D_MODEL), l)]
    h_shape = jax.ShapeDtypeStruct((tp, D_MODEL), jnp.float32)
    h_spec = pl.BlockSpec((tm, D_MODEL), row)
    if gnext is None:
        return pl.pallas_call(
            _ffn_out_last_kernel, out_shape=h_shape, grid=(tp // tm,), in_specs=in_specs,
            out_specs=h_spec, compiler_params=_cparams(("parallel",)), name="ffn_out_last",
        )(act, w, h, gpost), None
    return pl.pallas_call(
        _ffn_out_kernel,
        out_shape=(h_shape, jax.ShapeDtypeStruct((tp, D_MODEL), jnp.bfloat16)),
        grid=(tp // tm,),
        in_specs=in_specs + [_layer_spec((1, D_MODEL), l + 1)],
        out_specs=(h_spec, pl.BlockSpec((tm, D_MODEL), row)),
        compiler_params=_cparams(("parallel",)),
        name="ffn_out",
    )(act, w, h, gpost, gnext)


def _prep_small_weights(w_uq, w_ukv):
    bf = jnp.bfloat16
    depth = w_uq.shape[0]
    half = MLA_ROPE // 2
    wq = w_uq.reshape(depth, MLA_Q_RANK, MLA_HEADS, MLA_NOPE + MLA_ROPE)
    pe = wq[..., MLA_NOPE:]
    rot = jnp.concatenate([-pe[..., half:], pe[..., :half]], axis=-1)
    wq = jnp.concatenate([wq[..., :MLA_NOPE], pe, rot], axis=-1)
    wq = wq.reshape(depth, MLA_Q_RANK, MLA_HEADS * 2 * LANES).astype(bf)
    wkv = w_ukv.reshape(depth, MLA_KV_RANK, MLA_HEADS, MLA_NOPE + MLA_V)
    wkv = jnp.concatenate([wkv[..., :MLA_NOPE].reshape(depth, MLA_KV_RANK, -1),
                           wkv[..., MLA_NOPE:].reshape(depth, MLA_KV_RANK, -1)], axis=-1).astype(bf)
    return wq, wkv


def _rope_tables(lp):
    inv_freq = 1.0 / (ROPE_THETA ** (jnp.arange(0, MLA_ROPE, 2, dtype=jnp.float32) / MLA_ROPE))
    pos = jnp.maximum(jnp.arange(lp, dtype=jnp.float32) - PAD_FRONT, 0.0)
    ang = pos[:, None] * inv_freq[None, :]
    zeros = jnp.zeros((lp, LANES - MLA_ROPE), jnp.float32)
    cos_t = jnp.concatenate([jnp.cos(ang), jnp.cos(ang), zeros], axis=1)
    sin_t = jnp.concatenate([jnp.sin(ang), jnp.sin(ang), zeros], axis=1)
    return cos_t, sin_t


def _decay_selectors():
    src = jnp.arange(3 * LANES)
    dst = jnp.arange(FOX_HEADS * LANES)
    part, head_src = src // LANES, src % LANES - FLOG_LANE
    head_dst, lane_dst = dst // LANES, dst % LANES
    same_head = head_src[:, None] == head_dst[None, :]
    selq = (same_head & (lane_dst[None, :] == part[:, None])).astype(jnp.bfloat16)
    selk = -(same_head & (lane_dst[None, :] == part[:, None] + 3)).astype(jnp.bfloat16)
    return selq, selk


def kernel(x, meta, w_in, b_forget, g_q_lat, g_kv_lat, w_uq, w_ukv, conv_w, w_branch, w_out,
           w_ffn_in, w_ffn_out, g_mix_pre, g_mix_post, g_ffn_pre, g_ffn_post):
    batch, seq, _ = x.shape
    depth = w_in.shape[0]
    lp = PAD_FRONT + N_META + seq
    tp = batch * lp
    assert lp % ATT_BLOCK == 0

    tm_big = lp // 2
    tm_mid = lp // 4
    tm_small = lp // 8

    bf = jnp.bfloat16
    wq, wkv = _prep_small_weights(w_uq, w_ukv)
    wb, wo, wfo = w_branch.astype(bf), w_out.astype(bf), w_ffn_out.astype(bf)
    cos_t, sin_t = _rope_tables(lp)
    selq, selk = _decay_selectors()
    tri = (jnp.arange(ATT_BLOCK)[:, None] >= jnp.arange(ATT_BLOCK)[None, :]).astype(bf)
    bias = jnp.pad(b_forget.astype(jnp.float32),
                   ((0, 0), (FLOG_LANE, LANES - FLOG_LANE - FOX_HEADS)))[:, None, :]
    conv_w8 = jnp.pad(conv_w.astype(jnp.float32), ((0, 0), (0, 8 - CONV_K), (0, 0)))
    row2 = lambda g: g.astype(jnp.float32)[:, None, :]
    g_q, g_kv = row2(g_q_lat), row2(g_kv_lat)
    g_mpre, g_mpost, g_fpre, g_fpost = row2(g_mix_pre), row2(g_mix_post), row2(g_ffn_pre), row2(g_ffn_post)
    srow = jnp.ones((1, N_CONVF_TILES * IN_TILE), jnp.float32)
    srow = srow.at[:, 3 * CONV_WIDTH:3 * CONV_WIDTH + FOX_HEADS * FOX_HEAD_DIM].set(FOX_QSCALE)
    lanes_per_tile = IN_TILE // LANES
    fq_col, fk_col, fv_col = (3 * lanes_per_tile, 4 * lanes_per_tile, 5 * lanes_per_tile)

    h = jnp.concatenate([jnp.zeros((batch, PAD_FRONT, D_MODEL), x.dtype),
                         jnp.broadcast_to(meta[None].astype(x.dtype), (batch, N_META, D_MODEL)), x], axis=1)
    h = h.reshape(tp, D_MODEL)
    hn = _prenorm(h, g_mpre, 0, tm_mid)

    for l in range(depth):
        gates = _wproj(hn, w_in, srow, l, COL_GATE, N_GATE_TILES, True, tm_big)
        convf = _wproj(hn, w_in, srow, l, COL_CONV, N_CONVF_TILES, False, tm_big)
        lat, small = _latproj(hn, w_in, l, tm_big)
        qm, qe, km, ke, v_a = _mla_latent(lat, small, g_q, g_kv, wq, wkv, cos_t, sin_t, l, lp, tm_mid)
        o_a = _attention(qm, qe, km, ke, v_a, batch, lp, MLA_HEADS, shared_ke=True)
        aq, ak = _fox_decay(small, bias, tri, selq, selk, l, batch, lp)
        o_c = _attention(convf, aq, convf, ak, convf, batch, lp, FOX_HEADS, shared_ke=False,
                         qm_col=fq_col, km_col=fk_col, v_col=fv_col)
        o_b = _gated_conv(convf, conv_w8, l, batch, lp, tm_mid)
        h, hn = _merge_out(o_a, o_b, o_c, gates, wb, wo, h, g_mpost, g_fpre, l, tm_small)
        act = _ffn_in(hn, w_ffn_in, l, tm_big)
        gnext = g_mpre if l + 1 < depth else None
        h, hn = _ffn_out(act, wfo, h, g_fpost, gnext, l, tm_small)

    return h.reshape(batch, lp, D_MODEL)[:, PAD_FRONT + N_META:]
```

```python
import functools

import jax
import jax.numpy as jnp
from jax import lax
from jax.experimental import pallas as pl
from jax.experimental.pallas import tpu as pltpu

D_MODEL = 2048
N_META = 16
EPS = 1e-6
NEG_INF = -1e30
ROPE_THETA = 10000.0
MLA_HEADS = 8
MLA_Q_RANK = 512
MLA_KV_RANK = 512
MLA_NOPE = 128
MLA_ROPE = 64
MLA_V = 128
CONV_WIDTH = 1024
CONV_K = 3
FOX_HEADS = 8
FOX_HEAD_DIM = 128
N_BRANCH = 3
BRANCH_WIDTH = 1024
D_FF = 5632

LOG2E = 1.4426950408889634
MLA_QSCALE = (MLA_NOPE + MLA_ROPE) ** -0.5 * LOG2E
FOX_QSCALE = FOX_HEAD_DIM ** -0.5 * LOG2E

LANES = 128
MXU_N = 256
VMEM_LIMIT_BYTES = 56 * 1024 * 1024

ATT_BLOCK = 128
PAD_FRONT = (-N_META) % ATT_BLOCK

IN_TILE = 1024
COL_KPE = MLA_Q_RANK + MLA_KV_RANK
COL_CONV = COL_KPE + MLA_ROPE
N_CONVF_TILES = (3 * CONV_WIDTH + 3 * FOX_HEADS * FOX_HEAD_DIM) // IN_TILE
COL_FLOG = COL_CONV + N_CONVF_TILES * IN_TILE
COL_GATE = COL_FLOG + FOX_HEADS
N_GATE_TILES = N_BRANCH * D_MODEL // IN_TILE
SMALL_W = 2 * LANES
CAST_ROWS = 256
FF_TILE = 512


def _cparams(semantics):
    return pltpu.CompilerParams(dimension_semantics=semantics, vmem_limit_bytes=VMEM_LIMIT_BYTES)


def _rms(x, g):
    ms = jnp.mean(x * x, axis=-1, keepdims=True)
    return x * lax.rsqrt(ms + EPS) * g


def _sigmoid(x):
    return 1.0 / (1.0 + jnp.exp(-x))


def _layer_spec(shape, l):
    zeros = (0,) * len(shape)
    return pl.BlockSpec((None,) + tuple(shape), lambda *_: (l,) + zeros)


def _prenorm_kernel(h_ref, g_ref, o_ref):
    o_ref[...] = _rms(h_ref[...], g_ref[...]).astype(o_ref.dtype)


def _prenorm(h, g, l, tm):
    tp = h.shape[0]
    return pl.pallas_call(
        _prenorm_kernel,
        out_shape=jax.ShapeDtypeStruct((tp, D_MODEL), jnp.bfloat16),
        grid=(tp // tm,),
        in_specs=[pl.BlockSpec((tm, D_MODEL), lambda i: (i, 0)), _layer_spec((1, D_MODEL), l)],
        out_specs=pl.BlockSpec((tm, D_MODEL), lambda i: (i, 0)),
        compiler_params=_cparams(("parallel",)),
        name="prenorm",
    )(h, g)


_NT = (((1,), (1,)), ((), ()))


def _wproj_kernel(x_ref, w_ref, srow_ref, o_ref, w_scr, *, sigmoid):
    @pl.when(pl.program_id(1) == 0)
    def _():
        for r in range(0, IN_TILE, CAST_ROWS):
            w_scr[r:r + CAST_ROWS, :] = w_ref[0, r:r + CAST_ROWS, :].astype(jnp.bfloat16)

    x = x_ref[...]
    for c in range(0, IN_TILE, MXU_N):
        acc = lax.dot_general(x, w_scr[c:c + MXU_N, :], _NT, preferred_element_type=jnp.float32)
        y = _sigmoid(acc) if sigmoid else acc * srow_ref[:, c:c + MXU_N]
        o_ref[:, c:c + MXU_N] = y.astype(o_ref.dtype)


def _wproj(xn, w_in_t, srow, l, col0, n_tiles, sigmoid, tm):
    tp = xn.shape[0]
    assert col0 % 8 == 0
    return pl.pallas_call(
        functools.partial(_wproj_kernel, sigmoid=sigmoid),
        out_shape=jax.ShapeDtypeStruct((tp, n_tiles * IN_TILE), jnp.bfloat16),
        grid=(n_tiles, tp // tm),
        in_specs=[pl.BlockSpec((tm, D_MODEL), lambda j, i: (i, 0)),
                  pl.BlockSpec((pl.Element(1), pl.Element(IN_TILE), pl.Element(D_MODEL)),
                               lambda j, i: (l, pl.multiple_of(col0 + j * IN_TILE, 8), 0)),
                  pl.BlockSpec((1, IN_TILE), lambda j, i: (0, j))],
        out_specs=pl.BlockSpec((tm, IN_TILE), lambda j, i: (i, j)),
        scratch_shapes=[pltpu.VMEM((IN_TILE, D_MODEL), jnp.bfloat16)],
        compiler_params=_cparams(("arbitrary", "arbitrary")),
        name="wproj_gates" if sigmoid else "wproj_convf",
    )(xn, w_in_t, srow)


def _latproj_kernel(x_ref, wl_ref, wk_ref, wf_ref, lat_ref, small_ref, wl_scr, ws_scr):
    @pl.when(pl.program_id(0) == 0)
    def _():
        for r in range(0, IN_TILE, CAST_ROWS):
            wl_scr[r:r + CAST_ROWS, :] = wl_ref[r:r + CAST_ROWS, :].astype(jnp.bfloat16)
        kpe = wk_ref[...]
        half = MLA_ROPE // 2
        fill = jnp.zeros((SMALL_W - 2 * MLA_ROPE - FOX_HEADS, D_MODEL), jnp.float32)
        ws = jnp.concatenate([kpe, -kpe[half:], kpe[:half], wf_ref[...], fill], axis=0)
        ws_scr[...] = ws.astype(jnp.bfloat16)

    x = x_ref[...]
    for c in range(0, IN_TILE, MXU_N):
        lat_ref[:, c:c + MXU_N] = lax.dot_general(x, wl_scr[c:c + MXU_N, :], _NT,
                                                  preferred_element_type=jnp.float32).astype(lat_ref.dtype)
    small_ref[...] = lax.dot_general(x, ws_scr[...], _NT, preferred_element_type=jnp.float32)


def _latproj(xn, w_in_t, l, tm):
    tp = xn.shape[0]
    return pl.pallas_call(
        _latproj_kernel,
        out_shape=(jax.ShapeDtypeStruct((tp, IN_TILE), jnp.bfloat16),
                   jax.ShapeDtypeStruct((tp, SMALL_W), jnp.float32)),
        grid=(tp // tm,),
        in_specs=[pl.BlockSpec((tm, D_MODEL), lambda i: (i, 0)),
                  pl.BlockSpec((None, IN_TILE, D_MODEL), lambda i: (l, 0, 0), pipeline_mode=pl.Buffered(1)),
                  pl.BlockSpec((None, MLA_ROPE, D_MODEL), lambda i: (l, COL_KPE // MLA_ROPE, 0)),
                  pl.BlockSpec((None, FOX_HEADS, D_MODEL), lambda i: (l, COL_FLOG // FOX_HEADS, 0))],
        out_specs=(pl.BlockSpec((tm, IN_TILE), lambda i: (i, 0)),
                   pl.BlockSpec((tm, SMALL_W), lambda i: (i, 0))),
        scratch_shapes=[pltpu.VMEM((IN_TILE, D_MODEL), jnp.bfloat16),
                        pltpu.VMEM((SMALL_W, D_MODEL), jnp.bfloat16)],
        compiler_params=_cparams(("arbitrary",)),
        name="latproj",
    )(xn, w_in_t, w_in_t, w_in_t)


def _rope_pair(v, cos_t, sin_t):
    return v * cos_t + pltpu.roll(v, 64, 1) * sin_t


def _mla_latent_kernel(lat_ref, small_ref, gq_ref, gkv_ref, wq_ref, wkv_ref, cos_ref, sin_ref,
                       qm_ref, qe_ref, km_ref, ke_ref, v_ref):
    lat = lat_ref[...].astype(jnp.float32)
    cos_t = cos_ref[...]
    sin_t = sin_ref[...]
    cq = _rms(lat[:, :MLA_Q_RANK], gq_ref[...]).astype(jnp.bfloat16)
    ckv = _rms(lat[:, MLA_Q_RANK:], gkv_ref[...]).astype(jnp.bfloat16)
    q = jnp.dot(cq, wq_ref[...], preferred_element_type=jnp.float32)
    for h in range(MLA_HEADS):
        base = h * 2 * LANES
        qm_ref[:, h * LANES:(h + 1) * LANES] = (q[:, base:base + LANES] * MLA_QSCALE).astype(qm_ref.dtype)
        roped = _rope_pair(q[:, base + LANES:base + 2 * LANES], cos_t, sin_t)
        qe_ref[:, h * LANES:(h + 1) * LANES] = (roped * MLA_QSCALE).astype(qe_ref.dtype)
    kv = jnp.dot(ckv, wkv_ref[...], preferred_element_type=jnp.float32)
    km_ref[...] = kv[:, :MLA_HEADS * MLA_NOPE].astype(km_ref.dtype)
    v_ref[...] = kv[:, MLA_HEADS * MLA_NOPE:].astype(v_ref.dtype)
    ke_ref[...] = _rope_pair(small_ref[...], cos_t, sin_t).astype(ke_ref.dtype)


def _mla_latent(lat, small, gq, gkv, wq, wkv, cos_t, sin_t, l, lp, tm):
    tp = lat.shape[0]
    per_batch = lp // tm
    hw = MLA_HEADS * LANES
    bf = jnp.bfloat16
    row = lambda i: (i, 0)
    return pl.pallas_call(
        _mla_latent_kernel,
        out_shape=(jax.ShapeDtypeStruct((tp, hw), bf), jax.ShapeDtypeStruct((tp, hw), bf),
                   jax.ShapeDtypeStruct((tp, hw), bf), jax.ShapeDtypeStruct((tp, LANES), bf),
                   jax.ShapeDtypeStruct((tp, hw), bf)),
        grid=(tp // tm,),
        in_specs=[pl.BlockSpec((tm, IN_TILE), row),
                  pl.BlockSpec((tm, LANES), row),
                  _layer_spec((1, MLA_Q_RANK), l),
                  _layer_spec((1, MLA_KV_RANK), l),
                  _layer_spec((MLA_Q_RANK, 2 * hw), l),
                  _layer_spec((MLA_KV_RANK, 2 * hw), l),
                  pl.BlockSpec((tm, LANES), lambda i: (i % per_batch, 0)),
                  pl.BlockSpec((tm, LANES), lambda i: (i % per_batch, 0))],
        out_specs=(pl.BlockSpec((tm, hw), row), pl.BlockSpec((tm, hw), row),
                   pl.BlockSpec((tm, hw), row), pl.BlockSpec((tm, LANES), row),
                   pl.BlockSpec((tm, hw), row)),
        compiler_params=_cparams(("parallel",)),
        name="mla_latent",
    )(lat, small, gq, gkv, wq, wkv, cos_t, sin_t)


def _split3(x):
    hi = x.astype(jnp.bfloat16)
    r1 = x - hi.astype(jnp.float32)
    mid = r1.astype(jnp.bfloat16)
    lo = (r1 - mid.astype(jnp.float32)).astype(jnp.bfloat16)
    return hi, mid, lo


def _fox_decay_kernel(fl_ref, bias_ref, tri_ref, selq_ref, selk_ref, aq_ref, ak_ref, carry_ref):
    blk = pl.program_id(0)

    @pl.when(blk == 0)
    def _():
        carry_ref[...] = jnp.zeros_like(carry_ref)

    tri = tri_ref[...]
    rows = lax.broadcasted_iota(jnp.int32, (ATT_BLOCK, LANES), 0) + blk * ATT_BLOCK
    lane = lax.broadcasted_iota(jnp.int32, aq_ref.shape[1:], 1) % LANES
    for b in range(fl_ref.shape[0]):
        z = fl_ref[b] + bias_ref[...]
        log_f = jnp.minimum(z, 0.0) - jnp.log1p(jnp.exp(-jnp.abs(z)))
        log_f = jnp.where(rows >= PAD_FRONT, log_f * LOG2E, 0.0)
        c = carry_ref[b, 0:1, :]
        for part in _split3(log_f):
            c = c + jnp.dot(tri, part, preferred_element_type=jnp.float32)
        carry_ref[b] = jnp.broadcast_to(c[ATT_BLOCK - 1:ATT_BLOCK, :], carry_ref.shape[1:])
        parts = jnp.concatenate(_split3(c), axis=1)
        aq = jnp.dot(parts, selq_ref[...], preferred_element_type=jnp.float32)
        ak = jnp.dot(parts, selk_ref[...], preferred_element_type=jnp.float32)
        aq_ref[b] = jnp.where((lane >= 3) & (lane < 6), 1.0, aq).astype(aq_ref.dtype)
        ak_ref[b] = jnp.where(lane < 3, 1.0, ak).astype(ak_ref.dtype)


def _fox_decay(small, bias, tri, selq, selk, l, batch, lp):
    tp = small.shape[0]
    nblk = lp // ATT_BLOCK
    hw = FOX_HEADS * LANES
    const = lambda i: (0, 0)
    out = jax.ShapeDtypeStruct((batch, lp, hw), jnp.bfloat16)
    aq, ak = pl.pallas_call(
        _fox_decay_kernel,
        out_shape=(out, out),
        grid=(nblk,),
        in_specs=[pl.BlockSpec((batch, ATT_BLOCK, LANES), lambda i: (0, i, 1)),
                  _layer_spec((1, LANES), l),
                  pl.BlockSpec((ATT_BLOCK, ATT_BLOCK), const),
                  pl.BlockSpec((3 * LANES, hw), const),
                  pl.BlockSpec((3 * LANES, hw), const)],
        out_specs=(pl.BlockSpec((batch, ATT_BLOCK, hw), lambda i: (0, i, 0)),
                   pl.BlockSpec((batch, ATT_BLOCK, hw), lambda i: (0, i, 0))),
        scratch_shapes=[pltpu.VMEM((batch, 8, LANES), jnp.float32)],
        compiler_params=_cparams(("arbitrary",)),
        name="fox_decay",
    )(small.reshape(batch, lp, SMALL_W), bias, tri, selq, selk)
    return aq.reshape(tp, hw), ak.reshape(tp, hw)


Q_TILE = 256


def _attn_tile(qc_ref, kc_ref, v_ref, r0, tq):
    q = qc_ref[r0:r0 + tq, :]

    def scores(k0, k1):
        return lax.dot_general(q, kc_ref[k0:k1, :], (((1,), (1,)), ((), ())),
                               preferred_element_type=jnp.float32)

    col = lax.broadcasted_iota(jnp.int32, (tq, ATT_BLOCK), 1)
    ok = col >= PAD_FRONT
    if r0 == 0:
        ok = ok & (col <= lax.broadcasted_iota(jnp.int32, (tq, ATT_BLOCK), 0))
    pieces = [(jnp.where(ok, scores(0, ATT_BLOCK), NEG_INF), 0, ATT_BLOCK)]
    if r0 > 0:
        if r0 > ATT_BLOCK:
            pieces.append((scores(ATT_BLOCK, r0), ATT_BLOCK, r0))
        causal = (lax.broadcasted_iota(jnp.int32, (tq, tq), 1)
                  <= lax.broadcasted_iota(jnp.int32, (tq, tq), 0))
        pieces.append((jnp.where(causal, scores(r0, r0 + tq), NEG_INF), r0, r0 + tq))
    m = None
    for s, _, _ in pieces:
        pm = jnp.max(s, axis=1, keepdims=True)
        m = pm if m is None else jnp.maximum(m, pm)
    l = None
    acc = None
    for s, k0, k1 in pieces:
        p = jnp.exp2(s - m)
        ps = jnp.sum(p, axis=1, keepdims=True)
        pv = jnp.dot(p.astype(jnp.bfloat16), v_ref[k0:k1, :], preferred_element_type=jnp.float32)
        l = ps if l is None else l + ps
        acc = pv if acc is None else acc + pv
    return acc / l


def _attn_kernel(qm_ref, qe_ref, km_ref, ke_ref, v_ref, o_ref, qc_ref, kc_ref):
    qc_ref[:, :LANES] = qm_ref[...]
    qc_ref[:, LANES:] = qe_ref[...]
    kc_ref[:, :LANES] = km_ref[...]
    kc_ref[:, LANES:] = ke_ref[...]
    lp = o_ref.shape[0]
    tiles = [(0, ATT_BLOCK)] + [(r0, Q_TILE) for r0 in range(ATT_BLOCK, lp, Q_TILE)]
    for r0, tq in tiles:
        o_ref[r0:r0 + tq, :] = _attn_tile(qc_ref, kc_ref, v_ref, r0, tq).astype(o_ref.dtype)


def _attention(qm, qe, km, ke, v, batch, lp, n_heads, shared_ke, qm_col=0, km_col=0, v_col=0):
    assert (lp - ATT_BLOCK) % Q_TILE == 0
    tp = qm.shape[0]
    head = lambda b, h: (b, h)
    at = lambda col: (lambda b, h: (b, col + h))
    ke_map = (lambda b, h: (b, 0)) if shared_ke else head
    blockspec = lambda m: pl.BlockSpec((lp, LANES), m)
    return pl.pallas_call(
        _attn_kernel,
        out_shape=jax.ShapeDtypeStruct((tp, n_heads * LANES), jnp.bfloat16),
        grid=(batch, n_heads),
        in_specs=[blockspec(at(qm_col)), blockspec(head), blockspec(at(km_col)), blockspec(ke_map),
                  blockspec(at(v_col))],
        out_specs=blockspec(head),
        scratch_shapes=[pltpu.VMEM((lp, 2 * LANES), jnp.bfloat16),
                        pltpu.VMEM((lp, 2 * LANES), jnp.bfloat16)],
        compiler_params=_cparams(("parallel", "parallel")),
        name="attention",
    )(qm, qe, km, ke, v)


def _conv_kernel(b_ref, c_ref, x_ref, w_ref, o_ref, tail_ref):
    i = pl.program_id(1)
    tm = o_ref.shape[0]

    @pl.when(i == 0)
    def _():
        tail_ref[...] = jnp.zeros_like(tail_ref)

    u = c_ref[...].astype(jnp.float32) * x_ref[...].astype(jnp.float32)
    rows = lax.broadcasted_iota(jnp.int32, u.shape, 0)
    u = jnp.where(rows + i * tm >= PAD_FRONT, u, 0.0)
    prev1 = tail_ref[7:8, :]
    prev2 = tail_ref[6:7, :]
    u1 = jnp.where(rows == 0, prev1, pltpu.roll(u, 1, 0))
    u2 = jnp.where(rows == 0, prev2, jnp.where(rows == 1, prev1, pltpu.roll(u, 2, 0)))
    w = w_ref[...]
    conv = w[2:3, :] * u + w[1:2, :] * u1 + w[0:1, :] * u2
    o_ref[...] = (b_ref[...].astype(jnp.float32) * conv).astype(o_ref.dtype)
    tail_ref[...] = u[tm - 8:tm, :]


def _gated_conv(convf, conv_w, l, batch, lp, tm):
    tp = convf.shape[0]
    per_batch = lp // tm
    col = lambda t: (lambda b, i: (b * per_batch + i, t))
    return pl.pallas_call(
        _conv_kernel,
        out_shape=jax.ShapeDtypeStruct((tp, CONV_WIDTH), jnp.bfloat16),
        grid=(batch, per_batch),
        in_specs=[pl.BlockSpec((tm, IN_TILE), col(0)),
                  pl.BlockSpec((tm, IN_TILE), col(1)),
                  pl.BlockSpec((tm, IN_TILE), col(2)),
                  _layer_spec((8, CONV_WIDTH), l)],
        out_specs=pl.BlockSpec((tm, CONV_WIDTH), lambda b, i: (b * per_batch + i, 0)),
        scratch_shapes=[pltpu.VMEM((8, CONV_WIDTH), jnp.float32)],
        compiler_params=_cparams(("parallel", "arbitrary")),
        name="gated_conv",
    )(convf, convf, convf, conv_w)


def _residual_epilogue(y, h_ref, gpost_ref, gnext_ref, ho_ref, hn_ref):
    h_new = h_ref[...] + _rms(y, gpost_ref[...])
    ho_ref[...] = h_new
    if hn_ref is not None:
        hn_ref[...] = _rms(h_new, gnext_ref[...]).astype(hn_ref.dtype)


def _merge_out_kernel(oa_ref, ob_ref, oc_ref, gate_ref, wb_ref, wo_ref, h_ref, gpost_ref, gnext_ref,
                      ho_ref, hn_ref):
    merged = None
    for n, o_ref in enumerate((oa_ref, ob_ref, oc_ref)):
        y = jnp.dot(o_ref[...], wb_ref[n], preferred_element_type=jnp.float32)
        term = gate_ref[:, n * D_MODEL:(n + 1) * D_MODEL].astype(jnp.float32) * y
        merged = term if merged is None else merged + term
    mix = jnp.dot(merged.astype(jnp.bfloat16), wo_ref[...], preferred_element_type=jnp.float32)
    _residual_epilogue(mix, h_ref, gpost_ref, gnext_ref, ho_ref, hn_ref)


def _merge_out(o_a, o_b, o_c, gates, wb, wo, h, gpost, gnext, l, tm):
    tp = h.shape[0]
    row = lambda i: (i, 0)
    single = pl.Buffered(1)
    return pl.pallas_call(
        _merge_out_kernel,
        out_shape=(jax.ShapeDtypeStruct((tp, D_MODEL), jnp.float32),
                   jax.ShapeDtypeStruct((tp, D_MODEL), jnp.bfloat16)),
        grid=(tp // tm,),
        in_specs=[pl.BlockSpec((tm, BRANCH_WIDTH), row),
                  pl.BlockSpec((tm, BRANCH_WIDTH), row),
                  pl.BlockSpec((tm, BRANCH_WIDTH), row),
                  pl.BlockSpec((tm, N_BRANCH * D_MODEL), row),
                  pl.BlockSpec((None, N_BRANCH, BRANCH_WIDTH, D_MODEL), lambda i: (l, 0, 0, 0),
                               pipeline_mode=single),
                  pl.BlockSpec((None, D_MODEL, D_MODEL), lambda i: (l, 0, 0), pipeline_mode=single),
                  pl.BlockSpec((tm, D_MODEL), row),
                  _layer_spec((1, D_MODEL), l),
                  _layer_spec((1, D_MODEL), l)],
        out_specs=(pl.BlockSpec((tm, D_MODEL), row), pl.BlockSpec((tm, D_MODEL), row)),
        compiler_params=_cparams(("parallel",)),
        name="merge_out",
    )(o_a, o_b, o_c, gates, wb, wo, h, gpost, gnext)


def _ffn_in_kernel(x_ref, wg_ref, wu_ref, o_ref, wg_scr, wu_scr):
    @pl.when(pl.program_id(1) == 0)
    def _():
        for r in range(0, D_MODEL, CAST_ROWS):
            wg_scr[r:r + CAST_ROWS, :] = wg_ref[r:r + CAST_ROWS, :].astype(jnp.bfloat16)
            wu_scr[r:r + CAST_ROWS, :] = wu_ref[r:r + CAST_ROWS, :].astype(jnp.bfloat16)

    x = x_ref[...]
    for c in range(0, FF_TILE, MXU_N):
        g = jnp.dot(x, wg_scr[:, c:c + MXU_N], preferred_element_type=jnp.float32)
        u = jnp.dot(x, wu_scr[:, c:c + MXU_N], preferred_element_type=jnp.float32)
        o_ref[:, c:c + MXU_N] = (g * _sigmoid(g) * u).astype(o_ref.dtype)


def _ffn_in(xn, w, l, tm):
    tp = xn.shape[0]
    n_tiles = D_FF // FF_TILE
    return pl.pallas_call(
        _ffn_in_kernel,
        out_shape=jax.ShapeDtypeStruct((tp, D_FF), jnp.bfloat16),
        grid=(n_tiles, tp // tm),
        in_specs=[pl.BlockSpec((tm, D_MODEL), lambda j, i: (i, 0)),
                  pl.BlockSpec((None, D_MODEL, FF_TILE), lambda j, i: (l, 0, j)),
                  pl.BlockSpec((None, D_MODEL, FF_TILE), lambda j, i: (l, 0, j + n_tiles))],
        out_specs=pl.BlockSpec((tm, FF_TILE), lambda j, i: (i, j)),
        scratch_shapes=[pltpu.VMEM((D_MODEL, FF_TILE), jnp.bfloat16),
                        pltpu.VMEM((D_MODEL, FF_TILE), jnp.bfloat16)],
        compiler_params=_cparams(("arbitrary", "arbitrary")),
        name="ffn_in",
    )(xn, w, w)


def _ffn_out_kernel(a_ref, w_ref, h_ref, gpost_ref, gnext_ref, ho_ref, hn_ref):
    y = jnp.dot(a_ref[...], w_ref[...], preferred_element_type=jnp.float32)
    _residual_epilogue(y, h_ref, gpost_ref, gnext_ref, ho_ref, hn_ref)


def _ffn_out_last_kernel(a_ref, w_ref, h_ref, gpost_ref, ho_ref):
    y = jnp.dot(a_ref[...], w_ref[...], preferred_element_type=jnp.float32)
    _residual_epilogue(y, h_ref, gpost_ref, None, ho_ref, None)


def _ffn_out(act, w, h, gpost, gnext, l, tm):
    tp = h.shape[0]
    row = lambda i: (i, 0)
    in_specs = [pl.BlockSpec((tm, D_FF), row),
                pl.BlockSpec((None, D_FF, D_MODEL), lambda i: (l, 0, 0), pipeline_mode=pl.Buffered(1)),
                pl.BlockSpec((tm, D_MODEL), row),
                _layer_spec((1, D_MODEL), l)]
    h_shape = jax.ShapeDtypeStruct((tp, D_MODEL), jnp.float32)
    h_spec = pl.BlockSpec((tm, D_MODEL), row)
    if gnext is None:
        return pl.pallas_call(
            _ffn_out_last_kernel, out_shape=h_shape, grid=(tp // tm,), in_specs=in_specs,
            out_specs=h_spec, compiler_params=_cparams(("parallel",)), name="ffn_out_last",
        )(act, w, h, gpost), None
    return pl.pallas_call(
        _ffn_out_kernel,
        out_shape=(h_shape, jax.ShapeDtypeStruct((tp, D_MODEL), jnp.bfloat16)),
        grid=(tp // tm,),
        in_specs=in_specs + [_layer_spec((1, D_MODEL), l + 1)],
        out_specs=(h_spec, pl.BlockSpec((tm, D_MODEL), row)),
        compiler_params=_cparams(("parallel",)),
        name="ffn_out",
    )(act, w, h, gpost, gnext)


def _prep_small_weights(w_uq, w_ukv):
    bf = jnp.bfloat16
    depth = w_uq.shape[0]
    half = MLA_ROPE // 2
    wq = w_uq.reshape(depth, MLA_Q_RANK, MLA_HEADS, MLA_NOPE + MLA_ROPE)
    pe = wq[..., MLA_NOPE:]
    rot = jnp.concatenate([-pe[..., half:], pe[..., :half]], axis=-1)
    wq = jnp.concatenate([wq[..., :MLA_NOPE], pe, rot], axis=-1)
    wq = wq.reshape(depth, MLA_Q_RANK, MLA_HEADS * 2 * LANES).astype(bf)
    wkv = w_ukv.reshape(depth, MLA_KV_RANK, MLA_HEADS, MLA_NOPE + MLA_V)
    wkv = jnp.concatenate([wkv[..., :MLA_NOPE].reshape(depth, MLA_KV_RANK, -1),
                           wkv[..., MLA_NOPE:].reshape(depth, MLA_KV_RANK, -1)], axis=-1).astype(bf)
    return wq, wkv


def _rope_tables(lp):
    inv_freq = 1.0 / (ROPE_THETA ** (jnp.arange(0, MLA_ROPE, 2, dtype=jnp.float32) / MLA_ROPE))
    pos = jnp.maximum(jnp.arange(lp, dtype=jnp.float32) - PAD_FRONT, 0.0)
    ang = pos[:, None] * inv_freq[None, :]
    zeros = jnp.zeros((lp, LANES - MLA_ROPE), jnp.float32)
    cos_t = jnp.concatenate([jnp.cos(ang), jnp.cos(ang), zeros], axis=1)
    sin_t = jnp.concatenate([jnp.sin(ang), jnp.sin(ang), zeros], axis=1)
    return cos_t, sin_t


def _decay_selectors():
    src = jnp.arange(3 * LANES)
    dst = jnp.arange(FOX_HEADS * LANES)
    part, head_src = src // LANES, src % LANES
    head_dst, lane_dst = dst // LANES, dst % LANES
    same_head = head_src[:, None] == head_dst[None, :]
    selq = (same_head & (lane_dst[None, :] == part[:, None])).astype(jnp.bfloat16)
    selk = -(same_head & (lane_dst[None, :] == part[:, None] + 3)).astype(jnp.bfloat16)
    return selq, selk


def kernel(x, meta, w_in, b_forget, g_q_lat, g_kv_lat, w_uq, w_ukv, conv_w, w_branch, w_out,
           w_ffn_in, w_ffn_out, g_mix_pre, g_mix_post, g_ffn_pre, g_ffn_post):
    batch, seq, _ = x.shape
    depth = w_in.shape[0]
    lp = PAD_FRONT + N_META + seq
    tp = batch * lp
    assert lp % ATT_BLOCK == 0

    tm_big = lp // 2
    tm_mid = lp // 4
    tm_small = lp // 8

    bf = jnp.bfloat16
    wq, wkv = _prep_small_weights(w_uq, w_ukv)
    wb, wo, wfo = w_branch.astype(bf), w_out.astype(bf), w_ffn_out.astype(bf)
    cos_t, sin_t = _rope_tables(lp)
    selq, selk = _decay_selectors()
    tri = (jnp.arange(ATT_BLOCK)[:, None] >= jnp.arange(ATT_BLOCK)[None, :]).astype(bf)
    bias = jnp.pad(b_forget.astype(jnp.float32), ((0, 0), (0, LANES - FOX_HEADS)))[:, None, :]
    w_in_t = jnp.swapaxes(w_in, 1, 2)
    conv_w8 = jnp.pad(conv_w.astype(jnp.float32), ((0, 0), (0, 8 - CONV_K), (0, 0)))
    row2 = lambda g: g.astype(jnp.float32)[:, None, :]
    g_q, g_kv = row2(g_q_lat), row2(g_kv_lat)
    g_mpre, g_mpost, g_fpre, g_fpost = row2(g_mix_pre), row2(g_mix_post), row2(g_ffn_pre), row2(g_ffn_post)
    srow = jnp.ones((1, N_CONVF_TILES * IN_TILE), jnp.float32)
    srow = srow.at[:, 3 * CONV_WIDTH:3 * CONV_WIDTH + FOX_HEADS * FOX_HEAD_DIM].set(FOX_QSCALE)
    lanes_per_tile = IN_TILE // LANES
    fq_col, fk_col, fv_col = (3 * lanes_per_tile, 4 * lanes_per_tile, 5 * lanes_per_tile)

    h = jnp.concatenate([jnp.zeros((batch, PAD_FRONT, D_MODEL), x.dtype),
                         jnp.broadcast_to(meta[None].astype(x.dtype), (batch, N_META, D_MODEL)), x], axis=1)
    h = h.reshape(tp, D_MODEL)
    hn = _prenorm(h, g_mpre, 0, tm_mid)

    for l in range(depth):
        gates = _wproj(hn, w_in_t, srow, l, COL_GATE, N_GATE_TILES, True, tm_big)
        convf = _wproj(hn, w_in_t, srow, l, COL_CONV, N_CONVF_TILES, False, tm_big)
        lat, small = _latproj(hn, w_in_t, l, tm_big)
        qm, qe, km, ke, v_a = _mla_latent(lat, small, g_q, g_kv, wq, wkv, cos_t, sin_t, l, lp, tm_mid)
        o_a = _attention(qm, qe, km, ke, v_a, batch, lp, MLA_HEADS, shared_ke=True)
        aq, ak = _fox_decay(small, bias, tri, selq, selk, l, batch, lp)
        o_c = _attention(convf, aq, convf, ak, convf, batch, lp, FOX_HEADS, shared_ke=False,
                         qm_col=fq_col, km_col=fk_col, v_col=fv_col)
        o_b = _gated_conv(convf, conv_w8, l, batch, lp, tm_mid)
        h, hn = _merge_out(o_a, o_b, o_c, gates, wb, wo, h, g_mpost, g_fpre, l, tm_small)
        act = _ffn_in(hn, w_ffn_in, l, tm_big)
        gnext = g_mpre if l + 1 < depth else None
        h, hn = _ffn_out(act, wfo, h, g_fpost, gnext, l, tm_small)

    return h.reshape(batch, lp, D_MODEL)[:, PAD_FRONT + N_META:]
```

```python
import functools

import jax
import jax.numpy as jnp
from jax import lax
from jax.experimental import pallas as pl
from jax.experimental.pallas import tpu as pltpu

D_MODEL = 2048
N_META = 16
EPS = 1e-6
NEG_INF = -1e30
ROPE_THETA = 10000.0
MLA_HEADS = 8
MLA_Q_RANK = 512
MLA_KV_RANK = 512
MLA_NOPE = 128
MLA_ROPE = 64
MLA_V = 128
CONV_WIDTH = 1024
CONV_K = 3
FOX_HEADS = 8
FOX_HEAD_DIM = 128
N_BRANCH = 3
BRANCH_WIDTH = 1024
D_FF = 5632

LOG2E = 1.4426950408889634
MLA_QSCALE = (MLA_NOPE + MLA_ROPE) ** -0.5 * LOG2E
FOX_QSCALE = FOX_HEAD_DIM ** -0.5 * LOG2E

LANES = 128
MXU_N = 256
VMEM_LIMIT_BYTES = 56 * 1024 * 1024

BF16_ROWS = 16

IN_TILE = 1024
COL_KPE = MLA_Q_RANK + MLA_KV_RANK
COL_CONV = COL_KPE + MLA_ROPE
N_CONVF_TILES = (3 * CONV_WIDTH + 3 * FOX_HEADS * FOX_HEAD_DIM) // IN_TILE
COL_FLOG = COL_CONV + N_CONVF_TILES * IN_TILE
COL_GATE = COL_FLOG + FOX_HEADS
N_GATE_TILES = N_BRANCH * D_MODEL // IN_TILE
SMALL_W = 2 * LANES
CAST_ROWS = 256
FF_TILE = 512


def _row_tile(rows, target):
    return max(t for t in range(BF16_ROWS, target + 1, BF16_ROWS) if rows % t == 0)


def _cparams(semantics):
    return pltpu.CompilerParams(dimension_semantics=semantics, vmem_limit_bytes=VMEM_LIMIT_BYTES)


def _rms(x, g):
    ms = jnp.mean(x * x, axis=-1, keepdims=True)
    return x * lax.rsqrt(ms + EPS) * g


def _sigmoid(x):
    return 1.0 / (1.0 + jnp.exp(-x))


def _layer_spec(shape, l):
    zeros = (0,) * len(shape)
    return pl.BlockSpec((None,) + tuple(shape), lambda *_: (l,) + zeros)


def _prenorm_kernel(h_ref, g_ref, o_ref):
    o_ref[...] = _rms(h_ref[...], g_ref[...]).astype(o_ref.dtype)


def _prenorm(h, g, l, tm):
    tp = h.shape[0]
    return pl.pallas_call(
        _prenorm_kernel,
        out_shape=jax.ShapeDtypeStruct((tp, D_MODEL), jnp.bfloat16),
        grid=(tp // tm,),
        in_specs=[pl.BlockSpec((tm, D_MODEL), lambda i: (i, 0)), _layer_spec((1, D_MODEL), l)],
        out_specs=pl.BlockSpec((tm, D_MODEL), lambda i: (i, 0)),
        compiler_params=_cparams(("parallel",)),
        name="prenorm",
    )(h, g)


_NT = (((1,), (1,)), ((), ()))


def _wproj_kernel(x_ref, w_ref, srow_ref, o_ref, w_scr, *, sigmoid):
    @pl.when(pl.program_id(1) == 0)
    def _():
        for r in range(0, IN_TILE, CAST_ROWS):
            w_scr[r:r + CAST_ROWS, :] = w_ref[0, r:r + CAST_ROWS, :].astype(jnp.bfloat16)

    x = x_ref[...]
    chunk = lambda c: lax.dot_general(x, w_scr[c:c + MXU_N, :], _NT, preferred_element_type=jnp.float32)
    acc = chunk(0)
    for c in range(0, IN_TILE, MXU_N):
        nxt = chunk(c + MXU_N) if c + MXU_N < IN_TILE else None
        y = _sigmoid(acc) if sigmoid else acc * srow_ref[:, c:c + MXU_N]
        o_ref[:, c:c + MXU_N] = y.astype(o_ref.dtype)
        acc = nxt


def _wproj(xn, w_in_t, srow, l, col0, n_tiles, sigmoid, tm):
    tp = xn.shape[0]
    assert col0 % 8 == 0
    return pl.pallas_call(
        functools.partial(_wproj_kernel, sigmoid=sigmoid),
        out_shape=jax.ShapeDtypeStruct((tp, n_tiles * IN_TILE), jnp.bfloat16),
        grid=(n_tiles, tp // tm),
        in_specs=[pl.BlockSpec((tm, D_MODEL), lambda j, i: (i, 0)),
                  pl.BlockSpec((pl.Element(1), pl.Element(IN_TILE), pl.Element(D_MODEL)),
                               lambda j, i: (l, pl.multiple_of(col0 + j * IN_TILE, 8), 0)),
                  pl.BlockSpec((1, IN_TILE), lambda j, i: (0, j))],
        out_specs=pl.BlockSpec((tm, IN_TILE), lambda j, i: (i, j)),
        scratch_shapes=[pltpu.VMEM((IN_TILE, D_MODEL), jnp.bfloat16)],
        compiler_params=_cparams(("arbitrary", "arbitrary")),
        name="wproj_gates" if sigmoid else "wproj_convf",
    )(xn, w_in_t, srow)


def _latproj_kernel(x_ref, wl_ref, wk_ref, wf_ref, lat_ref, small_ref, wl_scr, ws_scr):
    @pl.when(pl.program_id(0) == 0)
    def _():
        for r in range(0, IN_TILE, CAST_ROWS):
            wl_scr[r:r + CAST_ROWS, :] = wl_ref[r:r + CAST_ROWS, :].astype(jnp.bfloat16)
        kpe = wk_ref[...]
        half = MLA_ROPE // 2
        fill = jnp.zeros((SMALL_W - 2 * MLA_ROPE - FOX_HEADS, D_MODEL), jnp.float32)
        ws = jnp.concatenate([kpe, -kpe[half:], kpe[:half], wf_ref[...], fill], axis=0)
        ws_scr[...] = ws.astype(jnp.bfloat16)

    x = x_ref[...]
    for c in range(0, IN_TILE, MXU_N):
        lat_ref[:, c:c + MXU_N] = lax.dot_general(x, wl_scr[c:c + MXU_N, :], _NT,
                                                  preferred_element_type=jnp.float32).astype(lat_ref.dtype)
    small_ref[...] = lax.dot_general(x, ws_scr[...], _NT, preferred_element_type=jnp.float32)


def _latproj(xn, w_in_t, l, tm):
    tp = xn.shape[0]
    return pl.pallas_call(
        _latproj_kernel,
        out_shape=(jax.ShapeDtypeStruct((tp, IN_TILE), jnp.bfloat16),
                   jax.ShapeDtypeStruct((tp, SMALL_W), jnp.float32)),
        grid=(tp // tm,),
        in_specs=[pl.BlockSpec((tm, D_MODEL), lambda i: (i, 0)),
                  pl.BlockSpec((None, IN_TILE, D_MODEL), lambda i: (l, 0, 0), pipeline_mode=pl.Buffered(1)),
                  pl.BlockSpec((None, MLA_ROPE, D_MODEL), lambda i: (l, COL_KPE // MLA_ROPE, 0)),
                  pl.BlockSpec((None, FOX_HEADS, D_MODEL), lambda i: (l, COL_FLOG // FOX_HEADS, 0))],
        out_specs=(pl.BlockSpec((tm, IN_TILE), lambda i: (i, 0)),
                   pl.BlockSpec((tm, SMALL_W), lambda i: (i, 0))),
        scratch_shapes=[pltpu.VMEM((IN_TILE, D_MODEL), jnp.bfloat16),
                        pltpu.VMEM((SMALL_W, D_MODEL), jnp.bfloat16)],
        compiler_params=_cparams(("arbitrary",)),
        name="latproj",
    )(xn, w_in_t, w_in_t, w_in_t)


def _rope_pair(v, cos_t, sin_t):
    return v * cos_t + pltpu.roll(v, 64, 1) * sin_t


def _mla_latent_kernel(lat_ref, small_ref, gq_ref, gkv_ref, wq_ref, wkv_ref, cos_ref, sin_ref,
                       qm_ref, qe_ref, km_ref, ke_ref, v_ref):
    lat = lat_ref[...].astype(jnp.float32)
    cos_t = cos_ref[...]
    sin_t = sin_ref[...]
    cq = _rms(lat[:, :MLA_Q_RANK], gq_ref[...]).astype(jnp.bfloat16)
    ckv = _rms(lat[:, MLA_Q_RANK:], gkv_ref[...]).astype(jnp.bfloat16)
    q = jnp.dot(cq, wq_ref[...], preferred_element_type=jnp.float32)
    for h in range(MLA_HEADS):
        base = h * 2 * LANES
        qm_ref[:, h * LANES:(h + 1) * LANES] = (q[:, base:base + LANES] * MLA_QSCALE).astype(qm_ref.dtype)
        roped = _rope_pair(q[:, base + LANES:base + 2 * LANES], cos_t, sin_t)
        qe_ref[:, h * LANES:(h + 1) * LANES] = (roped * MLA_QSCALE).astype(qe_ref.dtype)
    kv = jnp.dot(ckv, wkv_ref[...], preferred_element_type=jnp.float32)
    km_ref[...] = kv[:, :MLA_HEADS * MLA_NOPE].astype(km_ref.dtype)
    v_ref[...] = kv[:, MLA_HEADS * MLA_NOPE:].astype(v_ref.dtype)
    ke_ref[...] = _rope_pair(small_ref[...], cos_t, sin_t).astype(ke_ref.dtype)


def _mla_latent(lat, small, gq, gkv, wq, wkv, cos_t, sin_t, l, lp, tm):
    tp = lat.shape[0]
    per_batch = lp // tm
    hw = MLA_HEADS * LANES
    bf = jnp.bfloat16
    row = lambda i: (i, 0)
    return pl.pallas_call(
        _mla_latent_kernel,
        out_shape=(jax.ShapeDtypeStruct((tp, hw), bf), jax.ShapeDtypeStruct((tp, hw), bf),
                   jax.ShapeDtypeStruct((tp, hw), bf), jax.ShapeDtypeStruct((tp, LANES), bf),
                   jax.ShapeDtypeStruct((tp, hw), bf)),
        grid=(tp // tm,),
        in_specs=[pl.BlockSpec((tm, IN_TILE), row),
                  pl.BlockSpec((tm, LANES), row),
                  _layer_spec((1, MLA_Q_RANK), l),
                  _layer_spec((1, MLA_KV_RANK), l),
                  _layer_spec((MLA_Q_RANK, 2 * hw), l),
                  _layer_spec((MLA_KV_RANK, 2 * hw), l),
                  pl.BlockSpec((tm, LANES), lambda i: (i % per_batch, 0)),
                  pl.BlockSpec((tm, LANES), lambda i: (i % per_batch, 0))],
        out_specs=(pl.BlockSpec((tm, hw), row), pl.BlockSpec((tm, hw), row),
                   pl.BlockSpec((tm, hw), row), pl.BlockSpec((tm, LANES), row),
                   pl.BlockSpec((tm, hw), row)),
        compiler_params=_cparams(("parallel",)),
        name="mla_latent",
    )(lat, small, gq, gkv, wq, wkv, cos_t, sin_t)


def _split3(x):
    hi = x.astype(jnp.bfloat16)
    r1 = x - hi.astype(jnp.float32)
    mid = r1.astype(jnp.bfloat16)
    lo = (r1 - mid.astype(jnp.float32)).astype(jnp.bfloat16)
    return hi, mid, lo


def _fox_decay_kernel(fl_ref, bias_ref, tri_ref, selq_ref, selk_ref, aq_ref, ak_ref, carry_ref):
    blk = pl.program_id(0)

    @pl.when(blk == 0)
    def _():
        carry_ref[...] = jnp.zeros_like(carry_ref)

    tri = tri_ref[...]
    rows_per_blk = tri.shape[0]
    lane = lax.broadcasted_iota(jnp.int32, aq_ref.shape[1:], 1) % LANES
    for b in range(fl_ref.shape[0]):
        z = fl_ref[b] + bias_ref[...]
        log_f = (jnp.minimum(z, 0.0) - jnp.log1p(jnp.exp(-jnp.abs(z)))) * LOG2E
        c = carry_ref[b, 0:1, :]
        for part in _split3(log_f):
            c = c + jnp.dot(tri, part, preferred_element_type=jnp.float32)
        carry_ref[b] = jnp.broadcast_to(c[rows_per_blk - 1:rows_per_blk, :], carry_ref.shape[1:])
        parts = jnp.concatenate(_split3(c), axis=1)
        aq = jnp.dot(parts, selq_ref[...], preferred_element_type=jnp.float32)
        ak = jnp.dot(parts, selk_ref[...], preferred_element_type=jnp.float32)
        aq_ref[b] = jnp.where((lane >= 3) & (lane < 6), 1.0, aq).astype(aq_ref.dtype)
        ak_ref[b] = jnp.where(lane < 3, 1.0, ak).astype(ak_ref.dtype)


def _fox_decay(small, bias, tri, selq, selk, l, batch, lp):
    tp = small.shape[0]
    blk_rows = tri.shape[0]
    nblk = lp // blk_rows
    hw = FOX_HEADS * LANES
    const = lambda i: (0, 0)
    out = jax.ShapeDtypeStruct((batch, lp, hw), jnp.bfloat16)
    aq, ak = pl.pallas_call(
        _fox_decay_kernel,
        out_shape=(out, out),
        grid=(nblk,),
        in_specs=[pl.BlockSpec((batch, blk_rows, LANES), lambda i: (0, i, 1)),
                  _layer_spec((1, LANES), l),
                  pl.BlockSpec((blk_rows, blk_rows), const),
                  pl.BlockSpec((3 * LANES, hw), const),
                  pl.BlockSpec((3 * LANES, hw), const)],
        out_specs=(pl.BlockSpec((batch, blk_rows, hw), lambda i: (0, i, 0)),
                   pl.BlockSpec((batch, blk_rows, hw), lambda i: (0, i, 0))),
        scratch_shapes=[pltpu.VMEM((batch, 8, LANES), jnp.float32)],
        compiler_params=_cparams(("arbitrary",)),
        name="fox_decay",
    )(small.reshape(batch, lp, SMALL_W), bias, tri, selq, selk)
    return aq.reshape(tp, hw), ak.reshape(tp, hw)


Q_TILE = 256


def _attn_scores(qc_ref, kc_ref, r0, tq):
    q = qc_ref[r0:r0 + tq, :]

    def scores(k0, k1):
        return lax.dot_general(q, kc_ref[k0:k1, :], _NT, preferred_element_type=jnp.float32)

    pieces = []
    if r0 > 0:
        pieces.append((scores(0, N_META), 0, N_META))
        if r0 > N_META:
            pieces.append((scores(N_META, r0), N_META, r0))
    causal = (lax.broadcasted_iota(jnp.int32, (tq, tq), 1)
              <= lax.broadcasted_iota(jnp.int32, (tq, tq), 0))
    pieces.append((jnp.where(causal, scores(r0, r0 + tq), NEG_INF), r0, r0 + tq))
    return pieces


def _attn_softmax_pv(pieces, v_ref):
    m = None
    for s, _, _ in pieces:
        pm = jnp.max(s, axis=1, keepdims=True)
        m = pm if m is None else jnp.maximum(m, pm)
    l = None
    acc = None
    for s, k0, k1 in pieces:
        p = jnp.exp2(s - m)
        ps = jnp.sum(p, axis=1, keepdims=True)
        pv = jnp.dot(p.astype(jnp.bfloat16), v_ref[k0:k1, :], preferred_element_type=jnp.float32)
        l = ps if l is None else l + ps
        acc = pv if acc is None else acc + pv
    return acc / l


def _attn_kernel(qm_ref, qe_ref, km_ref, ke_ref, v_ref, o_ref, qc_ref, kc_ref):
    qc_ref[:, :LANES] = qm_ref[...]
    qc_ref[:, LANES:] = qe_ref[...]
    kc_ref[:, :LANES] = km_ref[...]
    kc_ref[:, LANES:] = ke_ref[...]
    lp = o_ref.shape[0]
    tiles = [(0, N_META)] + [(r0, Q_TILE) for r0 in range(N_META, lp, Q_TILE)]
    pieces = _attn_scores(qc_ref, kc_ref, *tiles[0])
    for t, (r0, tq) in enumerate(tiles):
        nxt = _attn_scores(qc_ref, kc_ref, *tiles[t + 1]) if t + 1 < len(tiles) else None
        o_ref[r0:r0 + tq, :] = _attn_softmax_pv(pieces, v_ref).astype(o_ref.dtype)
        pieces = nxt


def _attention(qm, qe, km, ke, v, batch, lp, n_heads, shared_ke, qm_col=0, km_col=0, v_col=0):
    assert (lp - N_META) % Q_TILE == 0
    tp = qm.shape[0]
    head = lambda b, h: (b, h)
    at = lambda col: (lambda b, h: (b, col + h))
    ke_map = (lambda b, h: (b, 0)) if shared_ke else head
    blockspec = lambda m: pl.BlockSpec((lp, LANES), m)
    return pl.pallas_call(
        _attn_kernel,
        out_shape=jax.ShapeDtypeStruct((tp, n_heads * LANES), jnp.bfloat16),
        grid=(batch, n_heads),
        in_specs=[blockspec(at(qm_col)), blockspec(head), blockspec(at(km_col)), blockspec(ke_map),
                  blockspec(at(v_col))],
        out_specs=blockspec(head),
        scratch_shapes=[pltpu.VMEM((lp, 2 * LANES), jnp.bfloat16),
                        pltpu.VMEM((lp, 2 * LANES), jnp.bfloat16)],
        compiler_params=_cparams(("parallel", "parallel")),
        name="attention",
    )(qm, qe, km, ke, v)


def _conv_kernel(b_ref, c_ref, x_ref, w_ref, o_ref, tail_ref):
    i = pl.program_id(1)
    tm = o_ref.shape[0]

    @pl.when(i == 0)
    def _():
        tail_ref[...] = jnp.zeros_like(tail_ref)

    u = c_ref[...].astype(jnp.float32) * x_ref[...].astype(jnp.float32)
    rows = lax.broadcasted_iota(jnp.int32, u.shape, 0)
    prev1 = tail_ref[7:8, :]
    prev2 = tail_ref[6:7, :]
    u1 = jnp.where(rows == 0, prev1, pltpu.roll(u, 1, 0))
    u2 = jnp.where(rows == 0, prev2, jnp.where(rows == 1, prev1, pltpu.roll(u, 2, 0)))
    w = w_ref[...]
    conv = w[2:3, :] * u + w[1:2, :] * u1 + w[0:1, :] * u2
    o_ref[...] = (b_ref[...].astype(jnp.float32) * conv).astype(o_ref.dtype)
    tail_ref[...] = u[tm - 8:tm, :]


def _gated_conv(convf, conv_w, l, batch, lp, tm):
    tp = convf.shape[0]
    per_batch = lp // tm
    col = lambda t: (lambda b, i: (b * per_batch + i, t))
    return pl.pallas_call(
        _conv_kernel,
        out_shape=jax.ShapeDtypeStruct((tp, CONV_WIDTH), jnp.bfloat16),
        grid=(batch, per_batch),
        in_specs=[pl.BlockSpec((tm, IN_TILE), col(0)),
                  pl.BlockSpec((tm, IN_TILE), col(1)),
                  pl.BlockSpec((tm, IN_TILE), col(2)),
                  _layer_spec((8, CONV_WIDTH), l)],
        out_specs=pl.BlockSpec((tm, CONV_WIDTH), lambda b, i: (b * per_batch + i, 0)),
        scratch_shapes=[pltpu.VMEM((8, CONV_WIDTH), jnp.float32)],
        compiler_params=_cparams(("parallel", "arbitrary")),
        name="gated_conv",
    )(convf, convf, convf, conv_w)


def _residual_epilogue(y, h_ref, gpost_ref, gnext_ref, ho_ref, hn_ref):
    h_new = h_ref[...] + _rms(y, gpost_ref[...])
    ho_ref[...] = h_new
    if hn_ref is not None:
        hn_ref[...] = _rms(h_new, gnext_ref[...]).astype(hn_ref.dtype)


def _merge_out_kernel(oa_ref, ob_ref, oc_ref, gate_ref, wb_ref, wo_ref, h_ref, gpost_ref, gnext_ref,
                      ho_ref, hn_ref):
    merged = None
    for n, o_ref in enumerate((oa_ref, ob_ref, oc_ref)):
        y = jnp.dot(o_ref[...], wb_ref[n], preferred_element_type=jnp.float32)
        term = gate_ref[:, n * D_MODEL:(n + 1) * D_MODEL].astype(jnp.float32) * y
        merged = term if merged is None else merged + term
    mix = jnp.dot(merged.astype(jnp.bfloat16), wo_ref[...], preferred_element_type=jnp.float32)
    _residual_epilogue(mix, h_ref, gpost_ref, gnext_ref, ho_ref, hn_ref)


def _merge_out(o_a, o_b, o_c, gates, wb, wo, h, gpost, gnext, l, tm):
    tp = h.shape[0]
    row = lambda i: (i, 0)
    single = pl.Buffered(1)
    return pl.pallas_call(
        _merge_out_kernel,
        out_shape=(jax.ShapeDtypeStruct((tp, D_MODEL), jnp.float32),
                   jax.ShapeDtypeStruct((tp, D_MODEL), jnp.bfloat16)),
        grid=(tp // tm,),
        in_specs=[pl.BlockSpec((tm, BRANCH_WIDTH), row),
                  pl.BlockSpec((tm, BRANCH_WIDTH), row),
                  pl.BlockSpec((tm, BRANCH_WIDTH), row),
                  pl.BlockSpec((tm, N_BRANCH * D_MODEL), row),
                  pl.BlockSpec((None, N_BRANCH, BRANCH_WIDTH, D_MODEL), lambda i: (l, 0, 0, 0),
                               pipeline_mode=single),
                  pl.BlockSpec((None, D_MODEL, D_MODEL), lambda i: (l, 0, 0), pipeline_mode=single),
                  pl.BlockSpec((tm, D_MODEL), row),
                  _layer_spec((1, D_MODEL), l),
                  _layer_spec((1, D_MODEL), l)],
        out_specs=(pl.BlockSpec((tm, D_MODEL), row), pl.BlockSpec((tm, D_MODEL), row)),
        compiler_params=_cparams(("parallel",)),
        name="merge_out",
    )(o_a, o_b, o_c, gates, wb, wo, h, gpost, gnext)


def _ffn_in_kernel(x_ref, wg_ref, wu_ref, o_ref, wg_scr, wu_scr):
    @pl.when(pl.program_id(1) == 0)
    def _():
        for r in range(0, D_MODEL, CAST_ROWS):
            wg_scr[r:r + CAST_ROWS, :] = wg_ref[r:r + CAST_ROWS, :].astype(jnp.bfloat16)
            wu_scr[r:r + CAST_ROWS, :] = wu_ref[r:r + CAST_ROWS, :].astype(jnp.bfloat16)

    x = x_ref[...]
    chunk = lambda c: (jnp.dot(x, wg_scr[:, c:c + MXU_N], preferred_element_type=jnp.float32),
                       jnp.dot(x, wu_scr[:, c:c + MXU_N], preferred_element_type=jnp.float32))
    gu = chunk(0)
    for c in range(0, FF_TILE, MXU_N):
        nxt = chunk(c + MXU_N) if c + MXU_N < FF_TILE else None
        g, u = gu
        o_ref[:, c:c + MXU_N] = (g * _sigmoid(g) * u).astype(o_ref.dtype)
        gu = nxt


def _ffn_in(xn, w, l, tm):
    tp = xn.shape[0]
    n_tiles = D_FF // FF_TILE
    return pl.pallas_call(
        _ffn_in_kernel,
        out_shape=jax.ShapeDtypeStruct((tp, D_FF), jnp.bfloat16),
        grid=(n_tiles, tp // tm),
        in_specs=[pl.BlockSpec((tm, D_MODEL), lambda j, i: (i, 0)),
                  pl.BlockSpec((None, D_MODEL, FF_TILE), lambda j, i: (l, 0, j)),
                  pl.BlockSpec((None, D_MODEL, FF_TILE), lambda j, i: (l, 0, j + n_tiles))],
        out_specs=pl.BlockSpec((tm, FF_TILE), lambda j, i: (i, j)),
        scratch_shapes=[pltpu.VMEM((D_MODEL, FF_TILE), jnp.bfloat16),
                        pltpu.VMEM((D_MODEL, FF_TILE), jnp.bfloat16)],
        compiler_params=_cparams(("arbitrary", "arbitrary")),
        name="ffn_in",
    )(xn, w, w)


def _ffn_out_kernel(a_ref, w_ref, h_ref, gpost_ref, gnext_ref, ho_ref, hn_ref):
    y = jnp.dot(a_ref[...], w_ref[...], preferred_element_type=jnp.float32)
    _residual_epilogue(y, h_ref, gpost_ref, gnext_ref, ho_ref, hn_ref)


def _ffn_out_last_kernel(a_ref, w_ref, h_ref, gpost_ref, ho_ref):
    y = jnp.dot(a_ref[...], w_ref[...], preferred_element_type=jnp.float32)
    _residual_epilogue(y, h_ref, gpost_ref, None, ho_ref, None)


def _ffn_out(act, w, h, gpost, gnext, l, tm):
    tp = h.shape[0]
    row = lambda i: (i, 0)
    in_specs = [pl.BlockSpec((tm, D_FF), row),
                pl.BlockSpec((None, D_FF, D_MODEL), lambda i: (l, 0, 0), pipeline_mode=pl.Buffered(1)),
                pl.BlockSpec((tm, D_MODEL), row),
                _layer_spec((1, D_MODEL), l)]
    h_shape = jax.ShapeDtypeStruct((tp, D_MODEL), jnp.float32)
    h_spec = pl.BlockSpec((tm, D_MODEL), row)
    if gnext is None:
        return pl.pallas_call(
            _ffn_out_last_kernel, out_shape=h_shape, grid=(tp // tm,), in_specs=in_specs,
            out_specs=h_spec, compiler_params=_cparams(("parallel",)), name="ffn_out_last",
        )(act, w, h, gpost), None
    return pl.pallas_call(
        _ffn_out_kernel,
        out_shape=(h_shape, jax.ShapeDtypeStruct((tp, D_MODEL), jnp.bfloat16)),
        grid=(tp // tm,),
        in_specs=in_specs + [_layer_spec((1, D_MODEL), l + 1)],
        out_specs=(h_spec, pl.BlockSpec((tm, D_MODEL), row)),
        compiler_params=_cparams(("parallel",)),
        name="ffn_out",
    )(act, w, h, gpost, gnext)


def _prep_small_weights(w_uq, w_ukv):
    bf = jnp.bfloat16
    depth = w_uq.shape[0]
    half = MLA_ROPE // 2
    wq = w_uq.reshape(depth, MLA_Q_RANK, MLA_HEADS, MLA_NOPE + MLA_ROPE)
    pe = wq[..., MLA_NOPE:]
    rot = jnp.concatenate([-pe[..., half:], pe[..., :half]], axis=-1)
    wq = jnp.concatenate([wq[..., :MLA_NOPE], pe, rot], axis=-1)
    wq = wq.reshape(depth, MLA_Q_RANK, MLA_HEADS * 2 * LANES).astype(bf)
    wkv = w_ukv.reshape(depth, MLA_KV_RANK, MLA_HEADS, MLA_NOPE + MLA_V)
    wkv = jnp.concatenate([wkv[..., :MLA_NOPE].reshape(depth, MLA_KV_RANK, -1),
                           wkv[..., MLA_NOPE:].reshape(depth, MLA_KV_RANK, -1)], axis=-1).astype(bf)
    return wq, wkv


def _rope_tables(lp):
    inv_freq = 1.0 / (ROPE_THETA ** (jnp.arange(0, MLA_ROPE, 2, dtype=jnp.float32) / MLA_ROPE))
    pos = jnp.arange(lp, dtype=jnp.float32)
    ang = pos[:, None] * inv_freq[None, :]
    zeros = jnp.zeros((lp, LANES - MLA_ROPE), jnp.float32)
    cos_t = jnp.concatenate([jnp.cos(ang), jnp.cos(ang), zeros], axis=1)
    sin_t = jnp.concatenate([jnp.sin(ang), jnp.sin(ang), zeros], axis=1)
    return cos_t, sin_t


def _decay_selectors():
    src = jnp.arange(3 * LANES)
    dst = jnp.arange(FOX_HEADS * LANES)
    part, head_src = src // LANES, src % LANES
    head_dst, lane_dst = dst // LANES, dst % LANES
    same_head = head_src[:, None] == head_dst[None, :]
    selq = (same_head & (lane_dst[None, :] == part[:, None])).astype(jnp.bfloat16)
    selk = -(same_head & (lane_dst[None, :] == part[:, None] + 3)).astype(jnp.bfloat16)
    return selq, selk


def kernel(x, meta, w_in, b_forget, g_q_lat, g_kv_lat, w_uq, w_ukv, conv_w, w_branch, w_out,
           w_ffn_in, w_ffn_out, g_mix_pre, g_mix_post, g_ffn_pre, g_ffn_post):
    batch, seq, _ = x.shape
    depth = w_in.shape[0]
    lp = N_META + seq
    tp = batch * lp
    tm_big = _row_tile(tp, 1400)
    tm_mid = _row_tile(lp, 700)
    tm_small = _row_tile(tp, 288)

    bf = jnp.bfloat16
    wq, wkv = _prep_small_weights(w_uq, w_ukv)
    wb, wo, wfo = w_branch.astype(bf), w_out.astype(bf), w_ffn_out.astype(bf)
    cos_t, sin_t = _rope_tables(lp)
    selq, selk = _decay_selectors()
    tri = (jnp.arange(tm_mid)[:, None] >= jnp.arange(tm_mid)[None, :]).astype(bf)
    bias = jnp.pad(b_forget.astype(jnp.float32), ((0, 0), (0, LANES - FOX_HEADS)))[:, None, :]
    w_in_t = jnp.swapaxes(w_in, 1, 2)
    conv_w8 = jnp.pad(conv_w.astype(jnp.float32), ((0, 0), (0, 8 - CONV_K), (0, 0)))
    row2 = lambda g: g.astype(jnp.float32)[:, None, :]
    g_q, g_kv = row2(g_q_lat), row2(g_kv_lat)
    g_mpre, g_mpost, g_fpre, g_fpost = row2(g_mix_pre), row2(g_mix_post), row2(g_ffn_pre), row2(g_ffn_post)
    srow = jnp.ones((1, N_CONVF_TILES * IN_TILE), jnp.float32)
    srow = srow.at[:, 3 * CONV_WIDTH:3 * CONV_WIDTH + FOX_HEADS * FOX_HEAD_DIM].set(FOX_QSCALE)
    lanes_per_tile = IN_TILE // LANES
    fq_col, fk_col, fv_col = (3 * lanes_per_tile, 4 * lanes_per_tile, 5 * lanes_per_tile)

    h = jnp.concatenate([jnp.broadcast_to(meta[None].astype(x.dtype), (batch, N_META, D_MODEL)), x], axis=1)
    h = h.reshape(tp, D_MODEL)
    hn = _prenorm(h, g_mpre, 0, tm_mid)

    for l in range(depth):
        gates = _wproj(hn, w_in_t, srow, l, COL_GATE, N_GATE_TILES, True, tm_big)
        convf = _wproj(hn, w_in_t, srow, l, COL_CONV, N_CONVF_TILES, False, tm_big)
        lat, small = _latproj(hn, w_in_t, l, tm_big)
        qm, qe, km, ke, v_a = _mla_latent(lat, small, g_q, g_kv, wq, wkv, cos_t, sin_t, l, lp, tm_mid)
        o_a = _attention(qm, qe, km, ke, v_a, batch, lp, MLA_HEADS, shared_ke=True)
        aq, ak = _fox_decay(small, bias, tri, selq, selk, l, batch, lp)
        o_c = _attention(convf, aq, convf, ak, convf, batch, lp, FOX_HEADS, shared_ke=False,
                         qm_col=fq_col, km_col=fk_col, v_col=fv_col)
        o_b = _gated_conv(convf, conv_w8, l, batch, lp, tm_mid)
        h, hn = _merge_out(o_a, o_b, o_c, gates, wb, wo, h, g_mpost, g_fpre, l, tm_small)
        act = _ffn_in(hn, w_ffn_in, l, tm_big)
        gnext = g_mpre if l + 1 < depth else None
        h, hn = _ffn_out(act, wfo, h, g_fpost, gnext, l, tm_small)

    return h.reshape(batch, lp, D_MODEL)[:, N_META:]
```

```python
import functools

import jax
import jax.numpy as jnp
from jax import lax
from jax.experimental import pallas as pl
from jax.experimental.pallas import tpu as pltpu

D_MODEL = 2048
N_META = 16
EPS = 1e-6
NEG_INF = -1e30
ROPE_THETA = 10000.0
MLA_HEADS = 8
MLA_Q_RANK = 512
MLA_KV_RANK = 512
MLA_NOPE = 128
MLA_ROPE = 64
MLA_V = 128
CONV_WIDTH = 1024
CONV_K = 3
FOX_HEADS = 8
FOX_HEAD_DIM = 128
N_BRANCH = 3
BRANCH_WIDTH = 1024
D_FF = 5632

LOG2E = 1.4426950408889634
MLA_QSCALE = (MLA_NOPE + MLA_ROPE) ** -0.5 * LOG2E
FOX_QSCALE = FOX_HEAD_DIM ** -0.5 * LOG2E

LANES = 128
MXU_N = 256
VMEM_LIMIT_BYTES = 56 * 1024 * 1024

BF16_ROWS = 16

IN_TILE = 1024
COL_KPE = MLA_Q_RANK + MLA_KV_RANK
COL_CONV = COL_KPE + MLA_ROPE
N_CONVF_TILES = (3 * CONV_WIDTH + 3 * FOX_HEADS * FOX_HEAD_DIM) // IN_TILE
COL_FLOG = COL_CONV + N_CONVF_TILES * IN_TILE
COL_GATE = COL_FLOG + FOX_HEADS
N_GATE_TILES = N_BRANCH * D_MODEL // IN_TILE
SMALL_W = 2 * LANES
CAST_ROWS = 256
FF_TILE = 512


def _row_tile(rows, target):
    return max(t for t in range(BF16_ROWS, target + 1, BF16_ROWS) if rows % t == 0)


def _cparams(semantics):
    return pltpu.CompilerParams(dimension_semantics=semantics, vmem_limit_bytes=VMEM_LIMIT_BYTES)


def _rms(x, g):
    ms = jnp.mean(x * x, axis=-1, keepdims=True)
    return x * lax.rsqrt(ms + EPS) * g


def _sigmoid(x):
    return 1.0 / (1.0 + jnp.exp(-x))


def _layer_spec(shape, l):
    zeros = (0,) * len(shape)
    return pl.BlockSpec((None,) + tuple(shape), lambda *_: (l,) + zeros)


def _prenorm_kernel(h_ref, g_ref, o_ref):
    o_ref[...] = _rms(h_ref[...], g_ref[...]).astype(o_ref.dtype)


def _prenorm(h, g, l, tm):
    tp = h.shape[0]
    return pl.pallas_call(
        _prenorm_kernel,
        out_shape=jax.ShapeDtypeStruct((tp, D_MODEL), jnp.bfloat16),
        grid=(tp // tm,),
        in_specs=[pl.BlockSpec((tm, D_MODEL), lambda i: (i, 0)), _layer_spec((1, D_MODEL), l)],
        out_specs=pl.BlockSpec((tm, D_MODEL), lambda i: (i, 0)),
        compiler_params=_cparams(("parallel",)),
        name="prenorm",
    )(h, g)


_NT = (((1,), (1,)), ((), ()))


def _wproj_kernel(x_ref, w_ref, srow_ref, o_ref, w_scr, *, sigmoid):
    @pl.when(pl.program_id(1) == 0)
    def _():
        for r in range(0, IN_TILE, CAST_ROWS):
            w_scr[r:r + CAST_ROWS, :] = w_ref[0, r:r + CAST_ROWS, :].astype(jnp.bfloat16)

    half = x_ref.shape[0] // 2
    for c in range(0, IN_TILE, MXU_N):
        for r in (0, half):
            acc = lax.dot_general(x_ref[r:r + half, :], w_scr[c:c + MXU_N, :], _NT,
                                  preferred_element_type=jnp.float32)
            y = _sigmoid(acc) if sigmoid else acc * srow_ref[:, c:c + MXU_N]
            o_ref[r:r + half, c:c + MXU_N] = y.astype(o_ref.dtype)


def _wproj(xn, w_in_t, srow, l, col0, n_tiles, sigmoid, tm):
    tp = xn.shape[0]
    assert col0 % 8 == 0
    return pl.pallas_call(
        functools.partial(_wproj_kernel, sigmoid=sigmoid),
        out_shape=jax.ShapeDtypeStruct((tp, n_tiles * IN_TILE), jnp.bfloat16),
        grid=(n_tiles, tp // tm),
        in_specs=[pl.BlockSpec((tm, D_MODEL), lambda j, i: (i, 0)),
                  pl.BlockSpec((pl.Element(1), pl.Element(IN_TILE), pl.Element(D_MODEL)),
                               lambda j, i: (l, pl.multiple_of(col0 + j * IN_TILE, 8), 0)),
                  pl.BlockSpec((1, IN_TILE), lambda j, i: (0, j))],
        out_specs=pl.BlockSpec((tm, IN_TILE), lambda j, i: (i, j)),
        scratch_shapes=[pltpu.VMEM((IN_TILE, D_MODEL), jnp.bfloat16)],
        compiler_params=_cparams(("arbitrary", "arbitrary")),
        name="wproj_gates" if sigmoid else "wproj_convf",
    )(xn, w_in_t, srow)


def _latproj_kernel(x_ref, wl_ref, wk_ref, wf_ref, lat_ref, small_ref, wl_scr, ws_scr):
    @pl.when(pl.program_id(0) == 0)
    def _():
        for r in range(0, IN_TILE, CAST_ROWS):
            wl_scr[r:r + CAST_ROWS, :] = wl_ref[r:r + CAST_ROWS, :].astype(jnp.bfloat16)
        kpe = wk_ref[...]
        half = MLA_ROPE // 2
        fill = jnp.zeros((SMALL_W - 2 * MLA_ROPE - FOX_HEADS, D_MODEL), jnp.float32)
        ws = jnp.concatenate([kpe, -kpe[half:], kpe[:half], wf_ref[...], fill], axis=0)
        ws_scr[...] = ws.astype(jnp.bfloat16)

    x = x_ref[...]
    for c in range(0, IN_TILE, MXU_N):
        lat_ref[:, c:c + MXU_N] = lax.dot_general(x, wl_scr[c:c + MXU_N, :], _NT,
                                                  preferred_element_type=jnp.float32).astype(lat_ref.dtype)
    small_ref[...] = lax.dot_general(x, ws_scr[...], _NT, preferred_element_type=jnp.float32)


def _latproj(xn, w_in_t, l, tm):
    tp = xn.shape[0]
    return pl.pallas_call(
        _latproj_kernel,
        out_shape=(jax.ShapeDtypeStruct((tp, IN_TILE), jnp.bfloat16),
                   jax.ShapeDtypeStruct((tp, SMALL_W), jnp.float32)),
        grid=(tp // tm,),
        in_specs=[pl.BlockSpec((tm, D_MODEL), lambda i: (i, 0)),
                  pl.BlockSpec((None, IN_TILE, D_MODEL), lambda i: (l, 0, 0), pipeline_mode=pl.Buffered(1)),
                  pl.BlockSpec((None, MLA_ROPE, D_MODEL), lambda i: (l, COL_KPE // MLA_ROPE, 0)),
                  pl.BlockSpec((None, FOX_HEADS, D_MODEL), lambda i: (l, COL_FLOG // FOX_HEADS, 0))],
        out_specs=(pl.BlockSpec((tm, IN_TILE), lambda i: (i, 0)),
                   pl.BlockSpec((tm, SMALL_W), lambda i: (i, 0))),
        scratch_shapes=[pltpu.VMEM((IN_TILE, D_MODEL), jnp.bfloat16),
                        pltpu.VMEM((SMALL_W, D_MODEL), jnp.bfloat16)],
        compiler_params=_cparams(("arbitrary",)),
        name="latproj",
    )(xn, w_in_t, w_in_t, w_in_t)


def _rope_pair(v, cos_t, sin_t):
    return v * cos_t + pltpu.roll(v, 64, 1) * sin_t


def _mla_latent_kernel(lat_ref, small_ref, gq_ref, gkv_ref, wq_ref, wkv_ref, cos_ref, sin_ref,
                       qm_ref, qe_ref, km_ref, ke_ref, v_ref):
    lat = lat_ref[...].astype(jnp.float32)
    cos_t = cos_ref[...]
    sin_t = sin_ref[...]
    cq = _rms(lat[:, :MLA_Q_RANK], gq_ref[...]).astype(jnp.bfloat16)
    ckv = _rms(lat[:, MLA_Q_RANK:], gkv_ref[...]).astype(jnp.bfloat16)
    q = jnp.dot(cq, wq_ref[...], preferred_element_type=jnp.float32)
    for h in range(MLA_HEADS):
        base = h * 2 * LANES
        qm_ref[:, h * LANES:(h + 1) * LANES] = (q[:, base:base + LANES] * MLA_QSCALE).astype(qm_ref.dtype)
        roped = _rope_pair(q[:, base + LANES:base + 2 * LANES], cos_t, sin_t)
        qe_ref[:, h * LANES:(h + 1) * LANES] = (roped * MLA_QSCALE).astype(qe_ref.dtype)
    kv = jnp.dot(ckv, wkv_ref[...], preferred_element_type=jnp.float32)
    km_ref[...] = kv[:, :MLA_HEADS * MLA_NOPE].astype(km_ref.dtype)
    v_ref[...] = kv[:, MLA_HEADS * MLA_NOPE:].astype(v_ref.dtype)
    ke_ref[...] = _rope_pair(small_ref[...], cos_t, sin_t).astype(ke_ref.dtype)


def _mla_latent(lat, small, gq, gkv, wq, wkv, cos_t, sin_t, l, lp, tm):
    tp = lat.shape[0]
    per_batch = lp // tm
    hw = MLA_HEADS * LANES
    bf = jnp.bfloat16
    row = lambda i: (i, 0)
    return pl.pallas_call(
        _mla_latent_kernel,
        out_shape=(jax.ShapeDtypeStruct((tp, hw), bf), jax.ShapeDtypeStruct((tp, hw), bf),
                   jax.ShapeDtypeStruct((tp, hw), bf), jax.ShapeDtypeStruct((tp, LANES), bf),
                   jax.ShapeDtypeStruct((tp, hw), bf)),
        grid=(tp // tm,),
        in_specs=[pl.BlockSpec((tm, IN_TILE), row),
                  pl.BlockSpec((tm, LANES), row),
                  _layer_spec((1, MLA_Q_RANK), l),
                  _layer_spec((1, MLA_KV_RANK), l),
                  _layer_spec((MLA_Q_RANK, 2 * hw), l),
                  _layer_spec((MLA_KV_RANK, 2 * hw), l),
                  pl.BlockSpec((tm, LANES), lambda i: (i % per_batch, 0)),
                  pl.BlockSpec((tm, LANES), lambda i: (i % per_batch, 0))],
        out_specs=(pl.BlockSpec((tm, hw), row), pl.BlockSpec((tm, hw), row),
                   pl.BlockSpec((tm, hw), row), pl.BlockSpec((tm, LANES), row),
                   pl.BlockSpec((tm, hw), row)),
        compiler_params=_cparams(("parallel",)),
        name="mla_latent",
    )(lat, small, gq, gkv, wq, wkv, cos_t, sin_t)


def _split3(x):
    hi = x.astype(jnp.bfloat16)
    r1 = x - hi.astype(jnp.float32)
    mid = r1.astype(jnp.bfloat16)
    lo = (r1 - mid.astype(jnp.float32)).astype(jnp.bfloat16)
    return hi, mid, lo


def _fox_decay_kernel(fl_ref, bias_ref, tri_ref, selq_ref, selk_ref, aq_ref, ak_ref, carry_ref):
    blk = pl.program_id(0)

    @pl.when(blk == 0)
    def _():
        carry_ref[...] = jnp.zeros_like(carry_ref)

    tri = tri_ref[...]
    rows_per_blk = tri.shape[0]
    lane = lax.broadcasted_iota(jnp.int32, aq_ref.shape[1:], 1) % DECAY_LANES
    for b in range(fl_ref.shape[0]):
        z = fl_ref[b] + bias_ref[...]
        log_f = (jnp.minimum(z, 0.0) - jnp.log1p(jnp.exp(-jnp.abs(z)))) * LOG2E
        c = carry_ref[b, 0:1, :]
        for part in _split3(log_f):
            c = c + jnp.dot(tri, part, preferred_element_type=jnp.float32)
        carry_ref[b] = jnp.broadcast_to(c[rows_per_blk - 1:rows_per_blk, :], carry_ref.shape[1:])
        parts = jnp.concatenate(_split3(c), axis=1)
        aq = jnp.dot(parts, selq_ref[...], preferred_element_type=jnp.float32)
        ak = jnp.dot(parts, selk_ref[...], preferred_element_type=jnp.float32)
        aq_ref[b] = jnp.where((lane >= 3) & (lane < 6), 1.0, aq).astype(aq_ref.dtype)
        ak_ref[b] = jnp.where(lane < 3, 1.0, ak).astype(ak_ref.dtype)


def _fox_decay(small, bias, tri, selq, selk, l, batch, lp):
    tp = small.shape[0]
    blk_rows = tri.shape[0]
    nblk = lp // blk_rows
    hw = FOX_HEADS * DECAY_LANES
    const = lambda i: (0, 0)
    out = jax.ShapeDtypeStruct((batch, lp, hw), jnp.bfloat16)
    aq, ak = pl.pallas_call(
        _fox_decay_kernel,
        out_shape=(out, out),
        grid=(nblk,),
        in_specs=[pl.BlockSpec((batch, blk_rows, LANES), lambda i: (0, i, 1)),
                  _layer_spec((1, LANES), l),
                  pl.BlockSpec((blk_rows, blk_rows), const),
                  pl.BlockSpec((3 * LANES, hw), const),
                  pl.BlockSpec((3 * LANES, hw), const)],
        out_specs=(pl.BlockSpec((batch, blk_rows, hw), lambda i: (0, i, 0)),
                   pl.BlockSpec((batch, blk_rows, hw), lambda i: (0, i, 0))),
        scratch_shapes=[pltpu.VMEM((batch, 8, LANES), jnp.float32)],
        compiler_params=_cparams(("arbitrary",)),
        name="fox_decay",
    )(small.reshape(batch, lp, SMALL_W), bias, tri, selq, selk)
    return aq.reshape(tp, hw), ak.reshape(tp, hw)


Q_TILE = 256
DECAY_LANES = LANES // FOX_HEADS


def _attn_scores(qc_ref, kc_ref, r0, tq):
    q = qc_ref[r0:r0 + tq, :]

    def scores(k0, k1):
        return lax.dot_general(q, kc_ref[k0:k1, :], _NT, preferred_element_type=jnp.float32)

    pieces = []
    if r0 > 0:
        pieces.append((scores(0, N_META), 0, N_META))
        if r0 > N_META:
            pieces.append((scores(N_META, r0), N_META, r0))
    causal = (lax.broadcasted_iota(jnp.int32, (tq, tq), 1)
              <= lax.broadcasted_iota(jnp.int32, (tq, tq), 0))
    pieces.append((jnp.where(causal, scores(r0, r0 + tq), NEG_INF), r0, r0 + tq))
    return pieces


def _attn_softmax_pv(pieces, v_ref):
    m = None
    for s, _, _ in pieces:
        pm = jnp.max(s, axis=1, keepdims=True)
        m = pm if m is None else jnp.maximum(m, pm)
    l = None
    acc = None
    for s, k0, k1 in pieces:
        p = jnp.exp2(s - m)
        ps = jnp.sum(p, axis=1, keepdims=True)
        pv = jnp.dot(p.astype(jnp.bfloat16), v_ref[k0:k1, :], preferred_element_type=jnp.float32)
        l = ps if l is None else l + ps
        acc = pv if acc is None else acc + pv
    return acc / l


def _attn_kernel(qm_ref, qe_ref, km_ref, ke_ref, v_ref, o_ref, qc_ref, kc_ref, *, qe_head_lanes):
    qe = qe_ref[...]
    if qe_head_lanes:
        lane0 = pl.program_id(1) * qe_head_lanes
        lane = lax.broadcasted_iota(jnp.int32, qe.shape, 1)
        qe = jnp.where((lane >= lane0) & (lane < lane0 + qe_head_lanes), qe, jnp.zeros_like(qe))
    qc_ref[:, :LANES] = qm_ref[...]
    qc_ref[:, LANES:] = qe
    kc_ref[:, :LANES] = km_ref[...]
    kc_ref[:, LANES:] = ke_ref[...]
    lp = o_ref.shape[0]
    tiles = [(0, N_META)] + [(r0, Q_TILE) for r0 in range(N_META, lp, Q_TILE)]
    pieces = _attn_scores(qc_ref, kc_ref, *tiles[0])
    for t, (r0, tq) in enumerate(tiles):
        nxt = _attn_scores(qc_ref, kc_ref, *tiles[t + 1]) if t + 1 < len(tiles) else None
        o_ref[r0:r0 + tq, :] = _attn_softmax_pv(pieces, v_ref).astype(o_ref.dtype)
        pieces = nxt


def _attention(qm, qe, km, ke, v, batch, lp, n_heads, qe_head_lanes=0, qm_col=0, km_col=0, v_col=0):
    assert (lp - N_META) % Q_TILE == 0
    tp = qm.shape[0]
    head = lambda b, h: (b, h)
    at = lambda col: (lambda b, h: (b, col + h))
    shared = lambda b, h: (b, 0)
    blockspec = lambda m: pl.BlockSpec((lp, LANES), m)
    return pl.pallas_call(
        functools.partial(_attn_kernel, qe_head_lanes=qe_head_lanes),
        out_shape=jax.ShapeDtypeStruct((tp, n_heads * LANES), jnp.bfloat16),
        grid=(batch, n_heads),
        in_specs=[blockspec(at(qm_col)), blockspec(shared if qe_head_lanes else head),
                  blockspec(at(km_col)), blockspec(shared), blockspec(at(v_col))],
        out_specs=blockspec(head),
        scratch_shapes=[pltpu.VMEM((lp, 2 * LANES), jnp.bfloat16),
                        pltpu.VMEM((lp, 2 * LANES), jnp.bfloat16)],
        compiler_params=_cparams(("parallel", "parallel")),
        name="attention",
    )(qm, qe, km, ke, v)


def _conv_kernel(b_ref, c_ref, x_ref, w_ref, o_ref, tail_ref):
    i = pl.program_id(1)
    tm = o_ref.shape[0]

    @pl.when(i == 0)
    def _():
        tail_ref[...] = jnp.zeros_like(tail_ref)

    u = c_ref[...].astype(jnp.float32) * x_ref[...].astype(jnp.float32)
    rows = lax.broadcasted_iota(jnp.int32, u.shape, 0)
    prev1 = tail_ref[7:8, :]
    prev2 = tail_ref[6:7, :]
    u1 = jnp.where(rows == 0, prev1, pltpu.roll(u, 1, 0))
    u2 = jnp.where(rows == 0, prev2, jnp.where(rows == 1, prev1, pltpu.roll(u, 2, 0)))
    w = w_ref[...]
    conv = w[2:3, :] * u + w[1:2, :] * u1 + w[0:1, :] * u2
    o_ref[...] = (b_ref[...].astype(jnp.float32) * conv).astype(o_ref.dtype)
    tail_ref[...] = u[tm - 8:tm, :]


def _gated_conv(convf, conv_w, l, batch, lp, tm):
    tp = convf.shape[0]
    per_batch = lp // tm
    col = lambda t: (lambda b, i: (b * per_batch + i, t))
    return pl.pallas_call(
        _conv_kernel,
        out_shape=jax.ShapeDtypeStruct((tp, CONV_WIDTH), jnp.bfloat16),
        grid=(batch, per_batch),
        in_specs=[pl.BlockSpec((tm, IN_TILE), col(0)),
                  pl.BlockSpec((tm, IN_TILE), col(1)),
                  pl.BlockSpec((tm, IN_TILE), col(2)),
                  _layer_spec((8, CONV_WIDTH), l)],
        out_specs=pl.BlockSpec((tm, CONV_WIDTH), lambda b, i: (b * per_batch + i, 0)),
        scratch_shapes=[pltpu.VMEM((8, CONV_WIDTH), jnp.float32)],
        compiler_params=_cparams(("parallel", "arbitrary")),
        name="gated_conv",
    )(convf, convf, convf, conv_w)


def _project_residual(a, w_ref, h_ref, gpost_ref, gnext_ref, ho_ref, hn_ref):
    y = jnp.dot(a, w_ref[...], preferred_element_type=jnp.float32)
    h_new = h_ref[...] + _rms(y, gpost_ref[...])
    ho_ref[...] = h_new
    if hn_ref is not None:
        hn_ref[...] = _rms(h_new, gnext_ref[...]).astype(hn_ref.dtype)


def _merge_out_kernel(oa_ref, ob_ref, oc_ref, gate_ref, wb_ref, wo_ref, h_ref, gpost_ref, gnext_ref,
                      ho_ref, hn_ref):
    merged = None
    for n, o_ref in enumerate((oa_ref, ob_ref, oc_ref)):
        y = jnp.dot(o_ref[...], wb_ref[n], preferred_element_type=jnp.float32)
        term = gate_ref[:, n * D_MODEL:(n + 1) * D_MODEL].astype(jnp.float32) * y
        merged = term if merged is None else merged + term
    _project_residual(merged.astype(jnp.bfloat16), wo_ref, h_ref, gpost_ref, gnext_ref, ho_ref, hn_ref)


def _merge_out(o_a, o_b, o_c, gates, wb, wo, h, gpost, gnext, l, tm):
    tp = h.shape[0]
    row = lambda i: (i, 0)
    single = pl.Buffered(1)
    return pl.pallas_call(
        _merge_out_kernel,
        out_shape=(jax.ShapeDtypeStruct((tp, D_MODEL), jnp.float32),
                   jax.ShapeDtypeStruct((tp, D_MODEL), jnp.bfloat16)),
        grid=(tp // tm,),
        in_specs=[pl.BlockSpec((tm, BRANCH_WIDTH), row),
                  pl.BlockSpec((tm, BRANCH_WIDTH), row),
                  pl.BlockSpec((tm, BRANCH_WIDTH), row),
                  pl.BlockSpec((tm, N_BRANCH * D_MODEL), row),
                  pl.BlockSpec((None, N_BRANCH, BRANCH_WIDTH, D_MODEL), lambda i: (l, 0, 0, 0),
                               pipeline_mode=single),
                  pl.BlockSpec((None, D_MODEL, D_MODEL), lambda i: (l, 0, 0), pipeline_mode=single),
                  pl.BlockSpec((tm, D_MODEL), row),
                  _layer_spec((1, D_MODEL), l),
                  _layer_spec((1, D_MODEL), l)],
        out_specs=(pl.BlockSpec((tm, D_MODEL), row), pl.BlockSpec((tm, D_MODEL), row)),
        compiler_params=_cparams(("parallel",)),
        name="merge_out",
    )(o_a, o_b, o_c, gates, wb, wo, h, gpost, gnext)


def _ffn_in_kernel(x_ref, wg_ref, wu_ref, o_ref, wg_scr, wu_scr):
    @pl.when(pl.program_id(1) == 0)
    def _():
        for r in range(0, D_MODEL, CAST_ROWS):
            wg_scr[r:r + CAST_ROWS, :] = wg_ref[r:r + CAST_ROWS, :].astype(jnp.bfloat16)
            wu_scr[r:r + CAST_ROWS, :] = wu_ref[r:r + CAST_ROWS, :].astype(jnp.bfloat16)

    x = x_ref[...]
    chunk = lambda c: (jnp.dot(x, wg_scr[:, c:c + MXU_N], preferred_element_type=jnp.float32),
                       jnp.dot(x, wu_scr[:, c:c + MXU_N], preferred_element_type=jnp.float32))
    gu = chunk(0)
    for c in range(0, FF_TILE, MXU_N):
        nxt = chunk(c + MXU_N) if c + MXU_N < FF_TILE else None
        g, u = gu
        o_ref[:, c:c + MXU_N] = (g * _sigmoid(g) * u).astype(o_ref.dtype)
        gu = nxt


def _ffn_in(xn, w, l, tm):
    tp = xn.shape[0]
    n_tiles = D_FF // FF_TILE
    return pl.pallas_call(
        _ffn_in_kernel,
        out_shape=jax.ShapeDtypeStruct((tp, D_FF), jnp.bfloat16),
        grid=(n_tiles, tp // tm),
        in_specs=[pl.BlockSpec((tm, D_MODEL), lambda j, i: (i, 0)),
                  pl.BlockSpec((None, D_MODEL, FF_TILE), lambda j, i: (l, 0, j)),
                  pl.BlockSpec((None, D_MODEL, FF_TILE), lambda j, i: (l, 0, j + n_tiles))],
        out_specs=pl.BlockSpec((tm, FF_TILE), lambda j, i: (i, j)),
        scratch_shapes=[pltpu.VMEM((D_MODEL, FF_TILE), jnp.bfloat16),
                        pltpu.VMEM((D_MODEL, FF_TILE), jnp.bfloat16)],
        compiler_params=_cparams(("arbitrary", "arbitrary")),
        name="ffn_in",
    )(xn, w, w)


def _ffn_out_kernel(a_ref, w_ref, h_ref, gpost_ref, gnext_ref, ho_ref, hn_ref):
    _project_residual(a_ref[...], w_ref, h_ref, gpost_ref, gnext_ref, ho_ref, hn_ref)


def _ffn_out_last_kernel(a_ref, w_ref, h_ref, gpost_ref, ho_ref):
    _project_residual(a_ref[...], w_ref, h_ref, gpost_ref, None, ho_ref, None)


def _ffn_out(act, w, h, gpost, gnext, l, tm):
    tp = h.shape[0]
    row = lambda i: (i, 0)
    in_specs = [pl.BlockSpec((tm, D_FF), row),
                pl.BlockSpec((None, D_FF, D_MODEL), lambda i: (l, 0, 0), pipeline_mode=pl.Buffered(1)),
                pl.BlockSpec((tm, D_MODEL), row),
                _layer_spec((1, D_MODEL), l)]
    h_shape = jax.ShapeDtypeStruct((tp, D_MODEL), jnp.float32)
    h_spec = pl.BlockSpec((tm, D_MODEL), row)
    if gnext is None:
        return pl.pallas_call(
            _ffn_out_last_kernel, out_shape=h_shape, grid=(tp // tm,), in_specs=in_specs,
            out_specs=h_spec, compiler_params=_cparams(("parallel",)), name="ffn_out_last",
        )(act, w, h, gpost), None
    return pl.pallas_call(
        _ffn_out_kernel,
        out_shape=(h_shape, jax.ShapeDtypeStruct((tp, D_MODEL), jnp.bfloat16)),
        grid=(tp // tm,),
        in_specs=in_specs + [_layer_spec((1, D_MODEL), l + 1)],
        out_specs=(h_spec, pl.BlockSpec((tm, D_MODEL), row)),
        compiler_params=_cparams(("parallel",)),
        name="ffn_out",
    )(act, w, h, gpost, gnext)


def _prep_small_weights(w_uq, w_ukv):
    bf = jnp.bfloat16
    depth = w_uq.shape[0]
    half = MLA_ROPE // 2
    wq = w_uq.reshape(depth, MLA_Q_RANK, MLA_HEADS, MLA_NOPE + MLA_ROPE)
    pe = wq[..., MLA_NOPE:]
    rot = jnp.concatenate([-pe[..., half:], pe[..., :half]], axis=-1)
    wq = jnp.concatenate([wq[..., :MLA_NOPE], pe, rot], axis=-1)
    wq = wq.reshape(depth, MLA_Q_RANK, MLA_HEADS * 2 * LANES).astype(bf)
    wkv = w_ukv.reshape(depth, MLA_KV_RANK, MLA_HEADS, MLA_NOPE + MLA_V)
    wkv = jnp.concatenate([wkv[..., :MLA_NOPE].reshape(depth, MLA_KV_RANK, -1),
                           wkv[..., MLA_NOPE:].reshape(depth, MLA_KV_RANK, -1)], axis=-1).astype(bf)
    return wq, wkv


def _rope_tables(lp):
    inv_freq = 1.0 / (ROPE_THETA ** (jnp.arange(0, MLA_ROPE, 2, dtype=jnp.float32) / MLA_ROPE))
    pos = jnp.arange(lp, dtype=jnp.float32)
    ang = pos[:, None] * inv_freq[None, :]
    zeros = jnp.zeros((lp, LANES - MLA_ROPE), jnp.float32)
    cos_t = jnp.concatenate([jnp.cos(ang), jnp.cos(ang), zeros], axis=1)
    sin_t = jnp.concatenate([jnp.sin(ang), jnp.sin(ang), zeros], axis=1)
    return cos_t, sin_t


def _decay_selectors():
    src = jnp.arange(3 * LANES)
    dst = jnp.arange(FOX_HEADS * DECAY_LANES)
    part, head_src = src // LANES, src % LANES
    head_dst, lane_dst = dst // DECAY_LANES, dst % DECAY_LANES
    same_head = head_src[:, None] == head_dst[None, :]
    selq = (same_head & (lane_dst[None, :] == part[:, None])).astype(jnp.bfloat16)
    selk = -(same_head & (lane_dst[None, :] == part[:, None] + 3)).astype(jnp.bfloat16)
    return selq, selk


def kernel(x, meta, w_in, b_forget, g_q_lat, g_kv_lat, w_uq, w_ukv, conv_w, w_branch, w_out,
           w_ffn_in, w_ffn_out, g_mix_pre, g_mix_post, g_ffn_pre, g_ffn_post):
    batch, seq, _ = x.shape
    depth = w_in.shape[0]
    lp = N_META + seq
    tp = batch * lp
    tm_big = _row_tile(tp, 1400)
    tm_mid = _row_tile(lp, 700)
    tm_small = _row_tile(tp, 288)

    bf = jnp.bfloat16
    wq, wkv = _prep_small_weights(w_uq, w_ukv)
    wb, wo, wfo = w_branch.astype(bf), w_out.astype(bf), w_ffn_out.astype(bf)
    cos_t, sin_t = _rope_tables(lp)
    selq, selk = _decay_selectors()
    tri = (jnp.arange(tm_mid)[:, None] >= jnp.arange(tm_mid)[None, :]).astype(bf)
    bias = jnp.pad(b_forget.astype(jnp.float32), ((0, 0), (0, LANES - FOX_HEADS)))[:, None, :]
    w_in_t = jnp.swapaxes(w_in, 1, 2)
    conv_w8 = jnp.pad(conv_w.astype(jnp.float32), ((0, 0), (0, 8 - CONV_K), (0, 0)))
    row2 = lambda g: g.astype(jnp.float32)[:, None, :]
    g_q, g_kv = row2(g_q_lat), row2(g_kv_lat)
    g_mpre, g_mpost, g_fpre, g_fpost = row2(g_mix_pre), row2(g_mix_post), row2(g_ffn_pre), row2(g_ffn_post)
    srow = jnp.ones((1, N_CONVF_TILES * IN_TILE), jnp.float32)
    srow = srow.at[:, 3 * CONV_WIDTH:3 * CONV_WIDTH + FOX_HEADS * FOX_HEAD_DIM].set(FOX_QSCALE)
    lanes_per_tile = IN_TILE // LANES
    fq_col, fk_col, fv_col = (3 * lanes_per_tile, 4 * lanes_per_tile, 5 * lanes_per_tile)

    h = jnp.concatenate([jnp.broadcast_to(meta[None].astype(x.dtype), (batch, N_META, D_MODEL)), x], axis=1)
    h = h.reshape(tp, D_MODEL)
    hn = _prenorm(h, g_mpre, 0, tm_mid)

    for l in range(depth):
        gates = _wproj(hn, w_in_t, srow, l, COL_GATE, N_GATE_TILES, True, tm_big)
        convf = _wproj(hn, w_in_t, srow, l, COL_CONV, N_CONVF_TILES, False, tm_big)
        lat, small = _latproj(hn, w_in_t, l, tm_big)
        qm, qe, km, ke, v_a = _mla_latent(lat, small, g_q, g_kv, wq, wkv, cos_t, sin_t, l, lp, tm_mid)
        o_a = _attention(qm, qe, km, ke, v_a, batch, lp, MLA_HEADS)
        aq, ak = _fox_decay(small, bias, tri, selq, selk, l, batch, lp)
        o_c = _attention(convf, aq, convf, ak, convf, batch, lp, FOX_HEADS, qe_head_lanes=DECAY_LANES,
                         qm_col=fq_col, km_col=fk_col, v_col=fv_col)
        o_b = _gated_conv(convf, conv_w8, l, batch, lp, tm_mid)
        h, hn = _merge_out(o_a, o_b, o_c, gates, wb, wo, h, g_mpost, g_fpre, l, tm_small)
        act = _ffn_in(hn, w_ffn_in, l, tm_big)
        gnext = g_mpre if l + 1 < depth else None
        h, hn = _ffn_out(act, wfo, h, g_fpost, gnext, l, tm_small)

    return h.reshape(batch, lp, D_MODEL)[:, N_META:]
```

```python
import functools
import math

import jax
import jax.numpy as jnp
from jax import lax
from jax.experimental import pallas as pl
from jax.experimental.pallas import tpu as pltpu

D_MODEL = 2048
N_META = 16
EPS = 1e-6
NEG_INF = -1e30
ROPE_THETA = 10000.0
MLA_HEADS = 8
MLA_Q_RANK = 512
MLA_KV_RANK = 512
MLA_NOPE = 128
MLA_ROPE = 64
MLA_V = 128
CONV_WIDTH = 1024
CONV_K = 3
FOX_HEADS = 8
FOX_HEAD_DIM = 128
N_BRANCH = 3
BRANCH_WIDTH = 1024
D_FF = 5632

LOG2E = 1.4426950408889634
MLA_QSCALE = (MLA_NOPE + MLA_ROPE) ** -0.5 * LOG2E
FOX_QSCALE = FOX_HEAD_DIM ** -0.5 * LOG2E

LANES = 128
MXU_N = 256
VMEM_LIMIT_BYTES = 56 * 1024 * 1024

BF16_ROWS = 16

IN_TILE = 1024
COL_KPE = MLA_Q_RANK + MLA_KV_RANK
COL_CONV = COL_KPE + MLA_ROPE
N_CONVF_TILES = (3 * CONV_WIDTH + 3 * FOX_HEADS * FOX_HEAD_DIM) // IN_TILE
COL_FLOG = COL_CONV + N_CONVF_TILES * IN_TILE
COL_GATE = COL_FLOG + FOX_HEADS
N_GATE_TILES = N_BRANCH * D_MODEL // IN_TILE
SMALL_W = 2 * LANES
CAST_ROWS = 256
FF_TILE = 512


def _row_tile(rows, target):
    return max(t for t in range(BF16_ROWS, target + 1, BF16_ROWS) if rows % t == 0)


def _cparams(semantics):
    return pltpu.CompilerParams(dimension_semantics=semantics, vmem_limit_bytes=VMEM_LIMIT_BYTES)


def _rms(x, g):
    ms = jnp.mean(x * x, axis=-1, keepdims=True)
    return x * lax.rsqrt(ms + EPS) * g


def _sigmoid(x):
    return 1.0 / (1.0 + jnp.exp(-x))


def _layer_spec(shape, l):
    zeros = (0,) * len(shape)
    return pl.BlockSpec((None,) + tuple(shape), lambda *_: (l,) + zeros)


def _prenorm_kernel(h_ref, g_ref, o_ref):
    o_ref[...] = _rms(h_ref[...], g_ref[...]).astype(o_ref.dtype)


def _prenorm(h, g, l, tm):
    tp = h.shape[0]
    return pl.pallas_call(
        _prenorm_kernel,
        out_shape=jax.ShapeDtypeStruct((tp, D_MODEL), jnp.bfloat16),
        grid=(tp // tm,),
        in_specs=[pl.BlockSpec((tm, D_MODEL), lambda i: (i, 0)), _layer_spec((1, D_MODEL), l)],
        out_specs=pl.BlockSpec((tm, D_MODEL), lambda i: (i, 0)),
        compiler_params=_cparams(("parallel",)),
        name="prenorm",
    )(h, g)


_NT = (((1,), (1,)), ((), ()))


def _wproj_kernel(x_ref, w_ref, srow_ref, o_ref, w_scr, *, sigmoid):
    @pl.when(pl.program_id(1) == 0)
    def _():
        for r in range(0, IN_TILE, CAST_ROWS):
            w_scr[r:r + CAST_ROWS, :] = w_ref[0, r:r + CAST_ROWS, :].astype(jnp.bfloat16)

    acc = lax.dot_general(x_ref[...], w_scr[...], _NT, preferred_element_type=jnp.float32)
    y = _sigmoid(acc) if sigmoid else acc * srow_ref[...]
    o_ref[...] = y.astype(o_ref.dtype)


def _wproj(xn, w_in_t, srow, l, col0, n_tiles, sigmoid, tm):
    tp = xn.shape[0]
    assert col0 % 8 == 0
    return pl.pallas_call(
        functools.partial(_wproj_kernel, sigmoid=sigmoid),
        out_shape=jax.ShapeDtypeStruct((tp, n_tiles * IN_TILE), jnp.bfloat16),
        grid=(n_tiles, tp // tm),
        in_specs=[pl.BlockSpec((tm, D_MODEL), lambda j, i: (i, 0)),
                  pl.BlockSpec((pl.Element(1), pl.Element(IN_TILE), pl.Element(D_MODEL)),
                               lambda j, i: (l, pl.multiple_of(col0 + j * IN_TILE, 8), 0)),
                  pl.BlockSpec((1, IN_TILE), lambda j, i: (0, j))],
        out_specs=pl.BlockSpec((tm, IN_TILE), lambda j, i: (i, j)),
        scratch_shapes=[pltpu.VMEM((IN_TILE, D_MODEL), jnp.bfloat16)],
        compiler_params=_cparams(("arbitrary", "arbitrary")),
        name="wproj_gates" if sigmoid else "wproj_convf",
    )(xn, w_in_t, srow)


def _latproj_kernel(x_ref, wl_ref, wk_ref, wf_ref, lat_ref, small_ref, wl_scr, ws_scr):
    @pl.when(pl.program_id(0) == 0)
    def _():
        for r in range(0, IN_TILE, CAST_ROWS):
            wl_scr[r:r + CAST_ROWS, :] = wl_ref[r:r + CAST_ROWS, :].astype(jnp.bfloat16)
        kpe = wk_ref[...]
        half = MLA_ROPE // 2
        fill = jnp.zeros((SMALL_W - 2 * MLA_ROPE - FOX_HEADS, D_MODEL), jnp.float32)
        ws = jnp.concatenate([kpe, -kpe[half:], kpe[:half], wf_ref[...], fill], axis=0)
        ws_scr[...] = ws.astype(jnp.bfloat16)

    x = x_ref[...]
    lat_ref[...] = lax.dot_general(x, wl_scr[...], _NT, preferred_element_type=jnp.float32).astype(lat_ref.dtype)
    small_ref[...] = lax.dot_general(x, ws_scr[...], _NT, preferred_element_type=jnp.float32)


def _latproj(xn, w_in_t, l, tm):
    tp = xn.shape[0]
    return pl.pallas_call(
        _latproj_kernel,
        out_shape=(jax.ShapeDtypeStruct((tp, IN_TILE), jnp.bfloat16),
                   jax.ShapeDtypeStruct((tp, SMALL_W), jnp.float32)),
        grid=(tp // tm,),
        in_specs=[pl.BlockSpec((tm, D_MODEL), lambda i: (i, 0)),
                  pl.BlockSpec((None, IN_TILE, D_MODEL), lambda i: (l, 0, 0), pipeline_mode=pl.Buffered(1)),
                  pl.BlockSpec((None, MLA_ROPE, D_MODEL), lambda i: (l, COL_KPE // MLA_ROPE, 0)),
                  pl.BlockSpec((None, FOX_HEADS, D_MODEL), lambda i: (l, COL_FLOG // FOX_HEADS, 0))],
        out_specs=(pl.BlockSpec((tm, IN_TILE), lambda i: (i, 0)),
                   pl.BlockSpec((tm, SMALL_W), lambda i: (i, 0))),
        scratch_shapes=[pltpu.VMEM((IN_TILE, D_MODEL), jnp.bfloat16),
                        pltpu.VMEM((SMALL_W, D_MODEL), jnp.bfloat16)],
        compiler_params=_cparams(("arbitrary",)),
        name="latproj",
    )(xn, w_in_t, w_in_t, w_in_t)


def _rope_pair(v, cos_t, sin_t):
    return v * cos_t + pltpu.roll(v, 64, 1) * sin_t


def _mla_latent_kernel(lat_ref, small_ref, gq_ref, gkv_ref, wq_ref, wkv_ref, cos_ref, sin_ref,
                       qm_ref, qe_ref, km_ref, ke_ref, v_ref):
    lat = lat_ref[...].astype(jnp.float32)
    cos_t = cos_ref[...]
    sin_t = sin_ref[...]
    cq = _rms(lat[:, :MLA_Q_RANK], gq_ref[...]).astype(jnp.bfloat16)
    ckv = _rms(lat[:, MLA_Q_RANK:], gkv_ref[...]).astype(jnp.bfloat16)
    q = jnp.dot(cq, wq_ref[...], preferred_element_type=jnp.float32)
    for h in range(MLA_HEADS):
        base = h * 2 * LANES
        qm_ref[:, h * LANES:(h + 1) * LANES] = (q[:, base:base + LANES] * MLA_QSCALE).astype(qm_ref.dtype)
        roped = _rope_pair(q[:, base + LANES:base + 2 * LANES], cos_t, sin_t)
        qe_ref[:, h * LANES:(h + 1) * LANES] = (roped * MLA_QSCALE).astype(qe_ref.dtype)
    kv = jnp.dot(ckv, wkv_ref[...], preferred_element_type=jnp.float32)
    km_ref[...] = kv[:, :MLA_HEADS * MLA_NOPE].astype(km_ref.dtype)
    v_ref[...] = kv[:, MLA_HEADS * MLA_NOPE:].astype(v_ref.dtype)
    ke_ref[...] = _rope_pair(small_ref[...], cos_t, sin_t).astype(ke_ref.dtype)


def _mla_latent(lat, small, gq, gkv, wq, wkv, cos_t, sin_t, l, lp, tm):
    tp = lat.shape[0]
    per_batch = lp // tm
    hw = MLA_HEADS * LANES
    bf = jnp.bfloat16
    row = lambda i: (i, 0)
    return pl.pallas_call(
        _mla_latent_kernel,
        out_shape=(jax.ShapeDtypeStruct((tp, hw), bf), jax.ShapeDtypeStruct((tp, hw), bf),
                   jax.ShapeDtypeStruct((tp, hw), bf), jax.ShapeDtypeStruct((tp, LANES), bf),
                   jax.ShapeDtypeStruct((tp, hw), bf)),
        grid=(tp // tm,),
        in_specs=[pl.BlockSpec((tm, IN_TILE), row),
                  pl.BlockSpec((tm, LANES), row),
                  _layer_spec((1, MLA_Q_RANK), l),
                  _layer_spec((1, MLA_KV_RANK), l),
                  _layer_spec((MLA_Q_RANK, 2 * hw), l),
                  _layer_spec((MLA_KV_RANK, 2 * hw), l),
                  pl.BlockSpec((tm, LANES), lambda i: (i % per_batch, 0)),
                  pl.BlockSpec((tm, LANES), lambda i: (i % per_batch, 0))],
        out_specs=(pl.BlockSpec((tm, hw), row), pl.BlockSpec((tm, hw), row),
                   pl.BlockSpec((tm, hw), row), pl.BlockSpec((tm, LANES), row),
                   pl.BlockSpec((tm, hw), row)),
        compiler_params=_cparams(("parallel",)),
        name="mla_latent",
    )(lat, small, gq, gkv, wq, wkv, cos_t, sin_t)


def _split3(x):
    hi = x.astype(jnp.bfloat16)
    r1 = x - hi.astype(jnp.float32)
    mid = r1.astype(jnp.bfloat16)
    lo = (r1 - mid.astype(jnp.float32)).astype(jnp.bfloat16)
    return hi, mid, lo


def _fox_decay_kernel(fl_ref, bias_ref, tri_ref, selq_ref, selk_ref, aq_ref, ak_ref, carry_ref):
    blk = pl.program_id(0)

    @pl.when(blk == 0)
    def _():
        carry_ref[...] = jnp.zeros_like(carry_ref)

    tri = tri_ref[...]
    rows_per_blk = tri.shape[0]
    lane = lax.broadcasted_iota(jnp.int32, aq_ref.shape[1:], 1) % DECAY_LANES
    for b in range(fl_ref.shape[0]):
        z = fl_ref[b] + bias_ref[...]
        log_f = (jnp.minimum(z, 0.0) - jnp.log1p(jnp.exp(-jnp.abs(z)))) * LOG2E
        c = carry_ref[b, 0:1, :]
        for part in _split3(log_f):
            c = c + jnp.dot(tri, part, preferred_element_type=jnp.float32)
        carry_ref[b] = jnp.broadcast_to(c[rows_per_blk - 1:rows_per_blk, :], carry_ref.shape[1:])
        parts = jnp.concatenate(_split3(c), axis=1)
        aq = jnp.dot(parts, selq_ref[...], preferred_element_type=jnp.float32)
        ak = jnp.dot(parts, selk_ref[...], preferred_element_type=jnp.float32)
        aq_ref[b] = jnp.where((lane >= 3) & (lane < 6), 1.0, aq).astype(aq_ref.dtype)
        ak_ref[b] = jnp.where(lane < 3, 1.0, ak).astype(ak_ref.dtype)


def _fox_decay(small, bias, tri, selq, selk, l, batch, lp):
    tp = small.shape[0]
    blk_rows = tri.shape[0]
    nblk = lp // blk_rows
    hw = FOX_HEADS * DECAY_LANES
    const = lambda i: (0, 0)
    out = jax.ShapeDtypeStruct((batch, lp, hw), jnp.bfloat16)
    aq, ak = pl.pallas_call(
        _fox_decay_kernel,
        out_shape=(out, out),
        grid=(nblk,),
        in_specs=[pl.BlockSpec((batch, blk_rows, LANES), lambda i: (0, i, 1)),
                  _layer_spec((1, LANES), l),
                  pl.BlockSpec((blk_rows, blk_rows), const),
                  pl.BlockSpec((3 * LANES, hw), const),
                  pl.BlockSpec((3 * LANES, hw), const)],
        out_specs=(pl.BlockSpec((batch, blk_rows, hw), lambda i: (0, i, 0)),
                   pl.BlockSpec((batch, blk_rows, hw), lambda i: (0, i, 0))),
        scratch_shapes=[pltpu.VMEM((batch, 8, LANES), jnp.float32)],
        compiler_params=_cparams(("arbitrary",)),
        name="fox_decay",
    )(small.reshape(batch, lp, SMALL_W), bias, tri, selq, selk)
    return aq.reshape(tp, hw), ak.reshape(tp, hw)


Q_TILE = 256
DECAY_LANES = LANES // FOX_HEADS


def _attn_scores(qc_ref, kc_ref, r0, tq):
    q = qc_ref[r0:r0 + tq, :]

    def scores(k0, k1):
        return lax.dot_general(q, kc_ref[k0:k1, :], _NT, preferred_element_type=jnp.float32)

    pieces = []
    if r0 > 0:
        pieces.append((scores(0, N_META), 0, N_META))
        if r0 > N_META:
            pieces.append((scores(N_META, r0), N_META, r0))
    causal = (lax.broadcasted_iota(jnp.int32, (tq, tq), 1)
              <= lax.broadcasted_iota(jnp.int32, (tq, tq), 0))
    pieces.append((jnp.where(causal, scores(r0, r0 + tq), NEG_INF), r0, r0 + tq))
    return pieces


def _attn_softmax_pv(pieces, v_ref):
    m = None
    for s, _, _ in pieces:
        pm = jnp.max(s, axis=1, keepdims=True)
        m = pm if m is None else jnp.maximum(m, pm)
    l = None
    acc = None
    for s, k0, k1 in pieces:
        p = jnp.exp2(s - m)
        ps = jnp.sum(p, axis=1, keepdims=True)
        pv = jnp.dot(p.astype(jnp.bfloat16), v_ref[k0:k1, :], preferred_element_type=jnp.float32)
        l = ps if l is None else l + ps
        acc = pv if acc is None else acc + pv
    return acc / l


def _attn_kernel(qm_ref, qe_ref, km_ref, ke_ref, v_ref, o_ref, qc_ref, kc_ref, *, qe_head_lanes):
    qe = qe_ref[...]
    if qe_head_lanes:
        lane0 = pl.program_id(1) * qe_head_lanes
        lane = lax.broadcasted_iota(jnp.int32, qe.shape, 1)
        qe = jnp.where((lane >= lane0) & (lane < lane0 + qe_head_lanes), qe, jnp.zeros_like(qe))
    qc_ref[:, :LANES] = qm_ref[...]
    qc_ref[:, LANES:] = qe
    kc_ref[:, :LANES] = km_ref[...]
    kc_ref[:, LANES:] = ke_ref[...]
    lp = o_ref.shape[0]
    tiles = [(0, N_META)] + [(r0, Q_TILE) for r0 in range(N_META, lp, Q_TILE)]
    pieces = _attn_scores(qc_ref, kc_ref, *tiles[0])
    for t, (r0, tq) in enumerate(tiles):
        nxt = _attn_scores(qc_ref, kc_ref, *tiles[t + 1]) if t + 1 < len(tiles) else None
        o_ref[r0:r0 + tq, :] = _attn_softmax_pv(pieces, v_ref).astype(o_ref.dtype)
        pieces = nxt


def _attention(qm, qe, km, ke, v, batch, lp, n_heads, qe_head_lanes=0, qm_col=0, km_col=0, v_col=0):
    assert (lp - N_META) % Q_TILE == 0
    tp = qm.shape[0]
    head = lambda b, h: (b, h)
    at = lambda col: (lambda b, h: (b, col + h))
    shared = lambda b, h: (b, 0)
    blockspec = lambda m: pl.BlockSpec((lp, LANES), m)
    return pl.pallas_call(
        functools.partial(_attn_kernel, qe_head_lanes=qe_head_lanes),
        out_shape=jax.ShapeDtypeStruct((tp, n_heads * LANES), jnp.bfloat16),
        grid=(batch, n_heads),
        in_specs=[blockspec(at(qm_col)), blockspec(shared if qe_head_lanes else head),
                  blockspec(at(km_col)), blockspec(shared), blockspec(at(v_col))],
        out_specs=blockspec(head),
        scratch_shapes=[pltpu.VMEM((lp, 2 * LANES), jnp.bfloat16),
                        pltpu.VMEM((lp, 2 * LANES), jnp.bfloat16)],
        compiler_params=_cparams(("parallel", "parallel")),
        name="attention",
    )(qm, qe, km, ke, v)


def _gated_conv_rows(b_ref, c_ref, x_ref, w_ref, tail_ref, rows_per_batch):
    i = pl.program_id(0)
    tm = b_ref.shape[0]

    @pl.when(i == 0)
    def _():
        tail_ref[...] = jnp.zeros_like(tail_ref)

    u = c_ref[...].astype(jnp.float32) * x_ref[...].astype(jnp.float32)
    rows = lax.broadcasted_iota(jnp.int32, u.shape, 0)
    start = (rows_per_batch - (i * tm) % rows_per_batch) % rows_per_batch
    prev1 = tail_ref[7:8, :]
    prev2 = tail_ref[6:7, :]
    u1 = jnp.where(rows == 0, prev1, pltpu.roll(u, 1, 0))
    u1 = jnp.where(rows == start, 0.0, u1)
    u2 = jnp.where(rows == 0, prev2, jnp.where(rows == 1, prev1, pltpu.roll(u, 2, 0)))
    u2 = jnp.where((rows == start) | (rows == start + 1), 0.0, u2)
    tail_ref[...] = u[tm - 8:tm, :]
    w = w_ref[...]
    return b_ref[...].astype(jnp.float32) * (w[2:3, :] * u + w[1:2, :] * u1 + w[0:1, :] * u2)


def _project_residual(a, w_ref, h_ref, gpost_ref, gnext_ref, ho_ref, hn_ref):
    y = jnp.dot(a, w_ref[...], preferred_element_type=jnp.float32)
    h_new = h_ref[...] + _rms(y, gpost_ref[...])
    ho_ref[...] = h_new
    if hn_ref is not None:
        hn_ref[...] = _rms(h_new, gnext_ref[...]).astype(hn_ref.dtype)


def _merge_out_kernel(oa_ref, cb_ref, cc_ref, cx_ref, oc_ref, gate_ref, cw_ref, wb_ref, wo_ref, h_ref,
                      gpost_ref, gnext_ref, ho_ref, hn_ref, tail_ref, *, rows_per_batch):
    o_b = _gated_conv_rows(cb_ref, cc_ref, cx_ref, cw_ref, tail_ref, rows_per_batch).astype(jnp.bfloat16)
    merged = None
    for n, o in enumerate((oa_ref[...], o_b, oc_ref[...])):
        y = jnp.dot(o, wb_ref[n], preferred_element_type=jnp.float32)
        term = gate_ref[:, n * D_MODEL:(n + 1) * D_MODEL].astype(jnp.float32) * y
        merged = term if merged is None else merged + term
    _project_residual(merged.astype(jnp.bfloat16), wo_ref, h_ref, gpost_ref, gnext_ref, ho_ref, hn_ref)


def _merge_out(o_a, convf, o_c, gates, conv_w, wb, wo, h, gpost, gnext, l, lp, tm):
    tp = h.shape[0]
    assert math.gcd(tm, lp) > 1
    row = lambda i: (i, 0)
    single = pl.Buffered(1)
    return pl.pallas_call(
        functools.partial(_merge_out_kernel, rows_per_batch=lp),
        out_shape=(jax.ShapeDtypeStruct((tp, D_MODEL), jnp.float32),
                   jax.ShapeDtypeStruct((tp, D_MODEL), jnp.bfloat16)),
        grid=(tp // tm,),
        in_specs=[pl.BlockSpec((tm, BRANCH_WIDTH), row),
                  pl.BlockSpec((tm, IN_TILE), lambda i: (i, 0)),
                  pl.BlockSpec((tm, IN_TILE), lambda i: (i, 1)),
                  pl.BlockSpec((tm, IN_TILE), lambda i: (i, 2)),
                  pl.BlockSpec((tm, BRANCH_WIDTH), row),
                  pl.BlockSpec((tm, N_BRANCH * D_MODEL), row),
                  _layer_spec((8, CONV_WIDTH), l),
                  pl.BlockSpec((None, N_BRANCH, BRANCH_WIDTH, D_MODEL), lambda i: (l, 0, 0, 0),
                               pipeline_mode=single),
                  pl.BlockSpec((None, D_MODEL, D_MODEL), lambda i: (l, 0, 0), pipeline_mode=single),
                  pl.BlockSpec((tm, D_MODEL), row),
                  _layer_spec((1, D_MODEL), l),
                  _layer_spec((1, D_MODEL), l)],
        out_specs=(pl.BlockSpec((tm, D_MODEL), row), pl.BlockSpec((tm, D_MODEL), row)),
        scratch_shapes=[pltpu.VMEM((8, CONV_WIDTH), jnp.float32)],
        compiler_params=_cparams(("arbitrary",)),
        name="merge_out",
    )(o_a, convf, convf, convf, o_c, gates, conv_w, wb, wo, h, gpost, gnext)


def _ffn_in_kernel(x_ref, wg_ref, wu_ref, o_ref, wg_scr, wu_scr):
    @pl.when(pl.program_id(1) == 0)
    def _():
        for r in range(0, D_MODEL, CAST_ROWS):
            wg_scr[r:r + CAST_ROWS, :] = wg_ref[r:r + CAST_ROWS, :].astype(jnp.bfloat16)
            wu_scr[r:r + CAST_ROWS, :] = wu_ref[r:r + CAST_ROWS, :].astype(jnp.bfloat16)

    x = x_ref[...]
    for c in range(0, FF_TILE, MXU_N):
        g = jnp.dot(x, wg_scr[:, c:c + MXU_N], preferred_element_type=jnp.float32)
        u = jnp.dot(x, wu_scr[:, c:c + MXU_N], preferred_element_type=jnp.float32)
        o_ref[:, c:c + MXU_N] = (g * _sigmoid(g) * u).astype(o_ref.dtype)


def _ffn_in(xn, w, l, tm):
    tp = xn.shape[0]
    n_tiles = D_FF // FF_TILE
    return pl.pallas_call(
        _ffn_in_kernel,
        out_shape=jax.ShapeDtypeStruct((tp, D_FF), jnp.bfloat16),
        grid=(n_tiles, tp // tm),
        in_specs=[pl.BlockSpec((tm, D_MODEL), lambda j, i: (i, 0)),
                  pl.BlockSpec((None, D_MODEL, FF_TILE), lambda j, i: (l, 0, j)),
                  pl.BlockSpec((None, D_MODEL, FF_TILE), lambda j, i: (l, 0, j + n_tiles))],
        out_specs=pl.BlockSpec((tm, FF_TILE), lambda j, i: (i, j)),
        scratch_shapes=[pltpu.VMEM((D_MODEL, FF_TILE), jnp.bfloat16),
                        pltpu.VMEM((D_MODEL, FF_TILE), jnp.bfloat16)],
        compiler_params=_cparams(("arbitrary", "arbitrary")),
        name="ffn_in",
    )(xn, w, w)


def _ffn_out_kernel(a_ref, w_ref, h_ref, gpost_ref, gnext_ref, ho_ref, hn_ref):
    _project_residual(a_ref[...], w_ref, h_ref, gpost_ref, gnext_ref, ho_ref, hn_ref)


def _ffn_out_last_kernel(a_ref, w_ref, h_ref, gpost_ref, ho_ref):
    _project_residual(a_ref[...], w_ref, h_ref, gpost_ref, None, ho_ref, None)


def _ffn_out(act, w, h, gpost, gnext, l, tm):
    tp = h.shape[0]
    row = lambda i: (i, 0)
    in_specs = [pl.BlockSpec((tm, D_FF), row),
                pl.BlockSpec((None, D_FF, D_MODEL), lambda i: (l, 0, 0), pipeline_mode=pl.Buffered(1)),
                pl.BlockSpec((tm, D_MODEL), row),
                _layer_spec((1, D_MODEL), l)]
    h_shape = jax.ShapeDtypeStruct((tp, D_MODEL), jnp.float32)
    h_spec = pl.BlockSpec((tm, D_MODEL), row)
    if gnext is None:
        return pl.pallas_call(
            _ffn_out_last_kernel, out_shape=h_shape, grid=(tp // tm,), in_specs=in_specs,
            out_specs=h_spec, compiler_params=_cparams(("parallel",)), name="ffn_out_last",
        )(act, w, h, gpost), None
    return pl.pallas_call(
        _ffn_out_kernel,
        out_shape=(h_shape, jax.ShapeDtypeStruct((tp, D_MODEL), jnp.bfloat16)),
        grid=(tp // tm,),
        in_specs=in_specs + [_layer_spec((1, D_MODEL), l + 1)],
        out_specs=(h_spec, pl.BlockSpec((tm, D_MODEL), row)),
        compiler_params=_cparams(("parallel",)),
        name="ffn_out",
    )(act, w, h, gpost, gnext)


def _prep_small_weights(w_uq, w_ukv):
    bf = jnp.bfloat16
    depth = w_uq.shape[0]
    half = MLA_ROPE // 2
    wq = w_uq.reshape(depth, MLA_Q_RANK, MLA_HEADS, MLA_NOPE + MLA_ROPE)
    pe = wq[..., MLA_NOPE:]
    rot = jnp.concatenate([-pe[..., half:], pe[..., :half]], axis=-1)
    wq = jnp.concatenate([wq[..., :MLA_NOPE], pe, rot], axis=-1)
    wq = wq.reshape(depth, MLA_Q_RANK, MLA_HEADS * 2 * LANES).astype(bf)
    wkv = w_ukv.reshape(depth, MLA_KV_RANK, MLA_HEADS, MLA_NOPE + MLA_V)
    wkv = jnp.concatenate([wkv[..., :MLA_NOPE].reshape(depth, MLA_KV_RANK, -1),
                           wkv[..., MLA_NOPE:].reshape(depth, MLA_KV_RANK, -1)], axis=-1).astype(bf)
    return wq, wkv


def _rope_tables(lp):
    inv_freq = 1.0 / (ROPE_THETA ** (jnp.arange(0, MLA_ROPE, 2, dtype=jnp.float32) / MLA_ROPE))
    pos = jnp.arange(lp, dtype=jnp.float32)
    ang = pos[:, None] * inv_freq[None, :]
    zeros = jnp.zeros((lp, LANES - MLA_ROPE), jnp.float32)
    cos_t = jnp.concatenate([jnp.cos(ang), jnp.cos(ang), zeros], axis=1)
    sin_t = jnp.concatenate([jnp.sin(ang), jnp.sin(ang), zeros], axis=1)
    return cos_t, sin_t


def _decay_selectors():
    src = jnp.arange(3 * LANES)
    dst = jnp.arange(FOX_HEADS * DECAY_LANES)
    part, head_src = src // LANES, src % LANES
    head_dst, lane_dst = dst // DECAY_LANES, dst % DECAY_LANES
    same_head = head_src[:, None] == head_dst[None, :]
    selq = (same_head & (lane_dst[None, :] == part[:, None])).astype(jnp.bfloat16)
    selk = -(same_head & (lane_dst[None, :] == part[:, None] + 3)).astype(jnp.bfloat16)
    return selq, selk


def kernel(x, meta, w_in, b_forget, g_q_lat, g_kv_lat, w_uq, w_ukv, conv_w, w_branch, w_out,
           w_ffn_in, w_ffn_out, g_mix_pre, g_mix_post, g_ffn_pre, g_ffn_post):
    batch, seq, _ = x.shape
    depth = w_in.shape[0]
    lp = N_META + seq
    tp = batch * lp
    tm_big = _row_tile(tp, 1400)
    tm_mid = _row_tile(lp, 700)
    tm_small = _row_tile(tp, 288)

    bf = jnp.bfloat16
    wq, wkv = _prep_small_weights(w_uq, w_ukv)
    wb, wo, wfo = w_branch.astype(bf), w_out.astype(bf), w_ffn_out.astype(bf)
    cos_t, sin_t = _rope_tables(lp)
    selq, selk = _decay_selectors()
    tri = (jnp.arange(tm_mid)[:, None] >= jnp.arange(tm_mid)[None, :]).astype(bf)
    bias = jnp.pad(b_forget.astype(jnp.float32), ((0, 0), (0, LANES - FOX_HEADS)))[:, None, :]
    w_in_t = jnp.swapaxes(w_in, 1, 2)
    conv_w8 = jnp.pad(conv_w.astype(jnp.float32), ((0, 0), (0, 8 - CONV_K), (0, 0)))
    row2 = lambda g: g.astype(jnp.float32)[:, None, :]
    g_q, g_kv = row2(g_q_lat), row2(g_kv_lat)
    g_mpre, g_mpost, g_fpre, g_fpost = row2(g_mix_pre), row2(g_mix_post), row2(g_ffn_pre), row2(g_ffn_post)
    srow = jnp.ones((1, N_CONVF_TILES * IN_TILE), jnp.float32)
    srow = srow.at[:, 3 * CONV_WIDTH:3 * CONV_WIDTH + FOX_HEADS * FOX_HEAD_DIM].set(FOX_QSCALE)
    lanes_per_tile = IN_TILE // LANES
    fq_col, fk_col, fv_col = (3 * lanes_per_tile, 4 * lanes_per_tile, 5 * lanes_per_tile)

    h = jnp.concatenate([jnp.broadcast_to(meta[None].astype(x.dtype), (batch, N_META, D_MODEL)), x], axis=1)
    h = h.reshape(tp, D_MODEL)
    hn = _prenorm(h, g_mpre, 0, tm_mid)

    for l in range(depth):
        gates = _wproj(hn, w_in_t, srow, l, COL_GATE, N_GATE_TILES, True, tm_big)
        convf = _wproj(hn, w_in_t, srow, l, COL_CONV, N_CONVF_TILES, False, tm_big)
        lat, small = _latproj(hn, w_in_t, l, tm_big)
        qm, qe, km, ke, v_a = _mla_latent(lat, small, g_q, g_kv, wq, wkv, cos_t, sin_t, l, lp, tm_mid)
        o_a = _attention(qm, qe, km, ke, v_a, batch, lp, MLA_HEADS)
        aq, ak = _fox_decay(small, bias, tri, selq, selk, l, batch, lp)
        o_c = _attention(convf, aq, convf, ak, convf, batch, lp, FOX_HEADS, qe_head_lanes=DECAY_LANES,
                         qm_col=fq_col, km_col=fk_col, v_col=fv_col)
        h, hn = _merge_out(o_a, convf, o_c, gates, conv_w8, wb, wo, h, g_mpost, g_fpre, l, lp, tm_small)
        act = _ffn_in(hn, w_ffn_in, l, tm_big)
        gnext = g_mpre if l + 1 < depth else None
        h, hn = _ffn_out(act, wfo, h, g_fpost, gnext, l, tm_small)

    return h.reshape(batch, lp, D_MODEL)[:, N_META:]
```

```python
import functools
import math

import jax
import jax.numpy as jnp
from jax import lax
from jax.experimental import pallas as pl
from jax.experimental.pallas import tpu as pltpu

D_MODEL = 2048
N_META = 16
EPS = 1e-6
NEG_INF = -1e30
ROPE_THETA = 10000.0
MLA_HEADS = 8
MLA_Q_RANK = 512
MLA_KV_RANK = 512
MLA_NOPE = 128
MLA_ROPE = 64
MLA_V = 128
CONV_WIDTH = 1024
CONV_K = 3
FOX_HEADS = 8
FOX_HEAD_DIM = 128
N_BRANCH = 3
BRANCH_WIDTH = 1024
D_FF = 5632

LOG2E = 1.4426950408889634
MLA_QSCALE = (MLA_NOPE + MLA_ROPE) ** -0.5 * LOG2E
FOX_QSCALE = FOX_HEAD_DIM ** -0.5 * LOG2E

LANES = 128
MXU_N = 256
VMEM_LIMIT_BYTES = 56 * 1024 * 1024

BF16_ROWS = 16

IN_TILE = 1024
COL_KPE = MLA_Q_RANK + MLA_KV_RANK
COL_CONV = COL_KPE + MLA_ROPE
N_CONVF_TILES = (3 * CONV_WIDTH + 3 * FOX_HEADS * FOX_HEAD_DIM) // IN_TILE
COL_FLOG = COL_CONV + N_CONVF_TILES * IN_TILE
COL_GATE = COL_FLOG + FOX_HEADS
N_GATE_TILES = N_BRANCH * D_MODEL // IN_TILE
SMALL_W = 2 * LANES
CAST_ROWS = 256
FF_TILE = 512


def _row_tile(rows, target):
    return max(t for t in range(BF16_ROWS, target + 1, BF16_ROWS) if rows % t == 0)


def _cparams(semantics):
    return pltpu.CompilerParams(dimension_semantics=semantics, vmem_limit_bytes=VMEM_LIMIT_BYTES)


def _rms(x, g):
    ms = jnp.mean(x * x, axis=-1, keepdims=True)
    return x * lax.rsqrt(ms + EPS) * g


def _sigmoid(x):
    return 1.0 / (1.0 + jnp.exp(-x))


def _layer_spec(shape, l):
    zeros = (0,) * len(shape)
    return pl.BlockSpec((None,) + tuple(shape), lambda *_: (l,) + zeros)


def _prenorm_kernel(h_ref, g_ref, o_ref):
    o_ref[...] = _rms(h_ref[...], g_ref[...]).astype(o_ref.dtype)


def _prenorm(h, g, l, tm):
    tp = h.shape[0]
    return pl.pallas_call(
        _prenorm_kernel,
        out_shape=jax.ShapeDtypeStruct((tp, D_MODEL), jnp.bfloat16),
        grid=(tp // tm,),
        in_specs=[pl.BlockSpec((tm, D_MODEL), lambda i: (i, 0)), _layer_spec((1, D_MODEL), l)],
        out_specs=pl.BlockSpec((tm, D_MODEL), lambda i: (i, 0)),
        compiler_params=_cparams(("parallel",)),
        name="prenorm",
    )(h, g)


_NT = (((1,), (1,)), ((), ()))


def _wproj_kernel(x_ref, w_ref, srow_ref, o_ref, w_scr, *, sigmoid):
    @pl.when(pl.program_id(1) == 0)
    def _():
        for r in range(0, IN_TILE, CAST_ROWS):
            w_scr[r:r + CAST_ROWS, :] = w_ref[0, r:r + CAST_ROWS, :].astype(jnp.bfloat16)

    acc = lax.dot_general(x_ref[...], w_scr[...], _NT, preferred_element_type=jnp.float32)
    y = _sigmoid(acc) if sigmoid else acc * srow_ref[...]
    o_ref[...] = y.astype(o_ref.dtype)


def _wproj(xn, w_in_t, srow, l, col0, n_tiles, sigmoid, tm):
    tp = xn.shape[0]
    assert col0 % 8 == 0
    return pl.pallas_call(
        functools.partial(_wproj_kernel, sigmoid=sigmoid),
        out_shape=jax.ShapeDtypeStruct((tp, n_tiles * IN_TILE), jnp.bfloat16),
        grid=(n_tiles, tp // tm),
        in_specs=[pl.BlockSpec((tm, D_MODEL), lambda j, i: (i, 0)),
                  pl.BlockSpec((pl.Element(1), pl.Element(IN_TILE), pl.Element(D_MODEL)),
                               lambda j, i: (l, pl.multiple_of(col0 + j * IN_TILE, 8), 0)),
                  pl.BlockSpec((1, IN_TILE), lambda j, i: (0, j))],
        out_specs=pl.BlockSpec((tm, IN_TILE), lambda j, i: (i, j)),
        scratch_shapes=[pltpu.VMEM((IN_TILE, D_MODEL), jnp.bfloat16)],
        compiler_params=_cparams(("arbitrary", "arbitrary")),
        name="wproj_gates" if sigmoid else "wproj_convf",
    )(xn, w_in_t, srow)


def _latproj_kernel(x_ref, wl_ref, wk_ref, wf_ref, lat_ref, small_ref, wl_scr, ws_scr):
    @pl.when(pl.program_id(0) == 0)
    def _():
        for r in range(0, IN_TILE, CAST_ROWS):
            wl_scr[r:r + CAST_ROWS, :] = wl_ref[r:r + CAST_ROWS, :].astype(jnp.bfloat16)
        kpe = wk_ref[...]
        half = MLA_ROPE // 2
        fill = jnp.zeros((SMALL_W - 2 * MLA_ROPE - FOX_HEADS, D_MODEL), jnp.float32)
        ws = jnp.concatenate([kpe, -kpe[half:], kpe[:half], wf_ref[...], fill], axis=0)
        ws_scr[...] = ws.astype(jnp.bfloat16)

    x = x_ref[...]
    lat_ref[...] = lax.dot_general(x, wl_scr[...], _NT, preferred_element_type=jnp.float32).astype(lat_ref.dtype)
    small_ref[...] = lax.dot_general(x, ws_scr[...], _NT, preferred_element_type=jnp.float32)


def _latproj(xn, w_in_t, l, tm):
    tp = xn.shape[0]
    return pl.pallas_call(
        _latproj_kernel,
        out_shape=(jax.ShapeDtypeStruct((tp, IN_TILE), jnp.bfloat16),
                   jax.ShapeDtypeStruct((tp, SMALL_W), jnp.float32)),
        grid=(tp // tm,),
        in_specs=[pl.BlockSpec((tm, D_MODEL), lambda i: (i, 0)),
                  pl.BlockSpec((None, IN_TILE, D_MODEL), lambda i: (l, 0, 0), pipeline_mode=pl.Buffered(1)),
                  pl.BlockSpec((None, MLA_ROPE, D_MODEL), lambda i: (l, COL_KPE // MLA_ROPE, 0)),
                  pl.BlockSpec((None, FOX_HEADS, D_MODEL), lambda i: (l, COL_FLOG // FOX_HEADS, 0))],
        out_specs=(pl.BlockSpec((tm, IN_TILE), lambda i: (i, 0)),
                   pl.BlockSpec((tm, SMALL_W), lambda i: (i, 0))),
        scratch_shapes=[pltpu.VMEM((IN_TILE, D_MODEL), jnp.bfloat16),
                        pltpu.VMEM((SMALL_W, D_MODEL), jnp.bfloat16)],
        compiler_params=_cparams(("arbitrary",)),
        name="latproj",
    )(xn, w_in_t, w_in_t, w_in_t)


def _rope_pair(v, cos_t, sin_t):
    return v * cos_t + pltpu.roll(v, 64, 1) * sin_t


def _mla_latent_kernel(lat_ref, small_ref, gq_ref, gkv_ref, wq_ref, wkv_ref, cos_ref, sin_ref,
                       qm_ref, qe_ref, km_ref, ke_ref, v_ref):
    lat = lat_ref[...].astype(jnp.float32)
    cos_t = cos_ref[...]
    sin_t = sin_ref[...]
    cq = _rms(lat[:, :MLA_Q_RANK], gq_ref[...]).astype(jnp.bfloat16)
    ckv = _rms(lat[:, MLA_Q_RANK:], gkv_ref[...]).astype(jnp.bfloat16)
    q = jnp.dot(cq, wq_ref[...], preferred_element_type=jnp.float32)
    for h in range(MLA_HEADS):
        base = h * 2 * LANES
        qm_ref[:, h * LANES:(h + 1) * LANES] = (q[:, base:base + LANES] * MLA_QSCALE).astype(qm_ref.dtype)
        roped = _rope_pair(q[:, base + LANES:base + 2 * LANES], cos_t, sin_t)
        qe_ref[:, h * LANES:(h + 1) * LANES] = (roped * MLA_QSCALE).astype(qe_ref.dtype)
    kv = jnp.dot(ckv, wkv_ref[...], preferred_element_type=jnp.float32)
    km_ref[...] = kv[:, :MLA_HEADS * MLA_NOPE].astype(km_ref.dtype)
    v_ref[...] = kv[:, MLA_HEADS * MLA_NOPE:].astype(v_ref.dtype)
    ke_ref[...] = _rope_pair(small_ref[...], cos_t, sin_t).astype(ke_ref.dtype)


def _mla_latent(lat, small, gq, gkv, wq, wkv, cos_t, sin_t, l, lp, tm):
    tp = lat.shape[0]
    per_batch = lp // tm
    hw = MLA_HEADS * LANES
    bf = jnp.bfloat16
    row = lambda i: (i, 0)
    return pl.pallas_call(
        _mla_latent_kernel,
        out_shape=(jax.ShapeDtypeStruct((tp, hw), bf), jax.ShapeDtypeStruct((tp, hw), bf),
                   jax.ShapeDtypeStruct((tp, hw), bf), jax.ShapeDtypeStruct((tp, LANES), bf),
                   jax.ShapeDtypeStruct((tp, hw), bf)),
        grid=(tp // tm,),
        in_specs=[pl.BlockSpec((tm, IN_TILE), row),
                  pl.BlockSpec((tm, LANES), row),
                  _layer_spec((1, MLA_Q_RANK), l),
                  _layer_spec((1, MLA_KV_RANK), l),
                  _layer_spec((MLA_Q_RANK, 2 * hw), l),
                  _layer_spec((MLA_KV_RANK, 2 * hw), l),
                  pl.BlockSpec((tm, LANES), lambda i: (i % per_batch, 0)),
                  pl.BlockSpec((tm, LANES), lambda i: (i % per_batch, 0))],
        out_specs=(pl.BlockSpec((tm, hw), row), pl.BlockSpec((tm, hw), row),
                   pl.BlockSpec((tm, hw), row), pl.BlockSpec((tm, LANES), row),
                   pl.BlockSpec((tm, hw), row)),
        compiler_params=_cparams(("parallel",)),
        name="mla_latent",
    )(lat, small, gq, gkv, wq, wkv, cos_t, sin_t)


def _split3(x):
    hi = x.astype(jnp.bfloat16)
    r1 = x - hi.astype(jnp.float32)
    mid = r1.astype(jnp.bfloat16)
    lo = (r1 - mid.astype(jnp.float32)).astype(jnp.bfloat16)
    return hi, mid, lo


def _fox_decay_kernel(fl_ref, bias_ref, tri_ref, selq_ref, selk_ref, aq_ref, ak_ref, carry_ref):
    blk = pl.program_id(0)

    @pl.when(blk == 0)
    def _():
        carry_ref[...] = jnp.zeros_like(carry_ref)

    tri = tri_ref[...]
    rows_per_blk = tri.shape[0]
    lane = lax.broadcasted_iota(jnp.int32, aq_ref.shape[1:], 1)
    z = fl_ref[0]
    for b in range(1, fl_ref.shape[0]):
        lane0 = b * FOX_HEADS
        z = jnp.where((lane >= lane0) & (lane < lane0 + FOX_HEADS), pltpu.roll(fl_ref[b], lane0, 1), z)
    z = z + bias_ref[...]
    log_f = (jnp.minimum(z, 0.0) - jnp.log1p(jnp.exp(-jnp.abs(z)))) * LOG2E
    c = carry_ref[0:1, :]
    for part in _split3(log_f):
        c = c + jnp.dot(tri, part, preferred_element_type=jnp.float32)
    carry_ref[...] = jnp.broadcast_to(c[rows_per_blk - 1:rows_per_blk, :], carry_ref.shape)
    parts = jnp.concatenate(_split3(c), axis=1)
    lane = lane % DECAY_LANES
    for b in range(fl_ref.shape[0]):
        aq = jnp.dot(parts, selq_ref[b], preferred_element_type=jnp.float32)
        ak = jnp.dot(parts, selk_ref[b], preferred_element_type=jnp.float32)
        aq_ref[b] = jnp.where((lane >= 3) & (lane < 6), 1.0, aq).astype(aq_ref.dtype)
        ak_ref[b] = jnp.where(lane < 3, 1.0, ak).astype(ak_ref.dtype)


def _fox_decay(small, bias, tri, selq, selk, l, batch, lp):
    tp = small.shape[0]
    blk_rows = tri.shape[0]
    nblk = lp // blk_rows
    hw = FOX_HEADS * DECAY_LANES
    assert batch * FOX_HEADS <= LANES and selq.shape == (batch, 3 * LANES, hw)
    const = lambda i: (0, 0)
    const3 = lambda i: (0, 0, 0)
    out = jax.ShapeDtypeStruct((batch, lp, hw), jnp.bfloat16)
    aq, ak = pl.pallas_call(
        _fox_decay_kernel,
        out_shape=(out, out),
        grid=(nblk,),
        in_specs=[pl.BlockSpec((batch, blk_rows, LANES), lambda i: (0, i, 1)),
                  _layer_spec((1, LANES), l),
                  pl.BlockSpec((blk_rows, blk_rows), const),
                  pl.BlockSpec((batch, 3 * LANES, hw), const3),
                  pl.BlockSpec((batch, 3 * LANES, hw), const3)],
        out_specs=(pl.BlockSpec((batch, blk_rows, hw), lambda i: (0, i, 0)),
                   pl.BlockSpec((batch, blk_rows, hw), lambda i: (0, i, 0))),
        scratch_shapes=[pltpu.VMEM((8, LANES), jnp.float32)],
        compiler_params=_cparams(("arbitrary",)),
        name="fox_decay",
    )(small.reshape(batch, lp, SMALL_W), bias, tri, selq, selk)
    return aq.reshape(tp, hw), ak.reshape(tp, hw)


Q_TILE = 256
DECAY_LANES = LANES // FOX_HEADS


def _attn_scores(qc_ref, kc_ref, r0, tq):
    q = qc_ref[r0:r0 + tq, :]

    def scores(k0, k1):
        return lax.dot_general(q, kc_ref[k0:k1, :], _NT, preferred_element_type=jnp.float32)

    pieces = []
    if r0 > 0:
        pieces.append((scores(0, N_META), 0, N_META))
        if r0 > N_META:
            pieces.append((scores(N_META, r0), N_META, r0))
    causal = (lax.broadcasted_iota(jnp.int32, (tq, tq), 1)
              <= lax.broadcasted_iota(jnp.int32, (tq, tq), 0))
    pieces.append((jnp.where(causal, scores(r0, r0 + tq), NEG_INF), r0, r0 + tq))
    return pieces


def _attn_softmax_pv(pieces, v_ref):
    m = None
    for s, _, _ in pieces:
        pm = jnp.max(s, axis=1, keepdims=True)
        m = pm if m is None else jnp.maximum(m, pm)
    l = None
    acc = None
    for s, k0, k1 in pieces:
        p = jnp.exp2(s - m)
        ps = jnp.sum(p, axis=1, keepdims=True)
        pv = jnp.dot(p.astype(jnp.bfloat16), v_ref[k0:k1, :], preferred_element_type=jnp.float32)
        l = ps if l is None else l + ps
        acc = pv if acc is None else acc + pv
    return acc / l


def _attn_kernel(qm_ref, qe_ref, km_ref, ke_ref, v_ref, o_ref, qc_ref, kc_ref, *, qe_head_lanes):
    qe = qe_ref[...]
    if qe_head_lanes:
        lane0 = pl.program_id(1) * qe_head_lanes
        lane = lax.broadcasted_iota(jnp.int32, qe.shape, 1)
        qe = jnp.where((lane >= lane0) & (lane < lane0 + qe_head_lanes), qe, jnp.zeros_like(qe))
    qc_ref[:, :LANES] = qm_ref[...]
    qc_ref[:, LANES:] = qe
    kc_ref[:, :LANES] = km_ref[...]
    kc_ref[:, LANES:] = ke_ref[...]
    lp = o_ref.shape[0]
    tiles = [(0, N_META)] + [(r0, Q_TILE) for r0 in range(N_META, lp, Q_TILE)]
    pieces = _attn_scores(qc_ref, kc_ref, *tiles[0])
    for t, (r0, tq) in enumerate(tiles):
        nxt = _attn_scores(qc_ref, kc_ref, *tiles[t + 1]) if t + 1 < len(tiles) else None
        o_ref[r0:r0 + tq, :] = _attn_softmax_pv(pieces, v_ref).astype(o_ref.dtype)
        pieces = nxt


def _attention(qm, qe, km, ke, v, batch, lp, n_heads, qe_head_lanes=0, qm_col=0, km_col=0, v_col=0):
    assert (lp - N_META) % Q_TILE == 0
    tp = qm.shape[0]
    head = lambda b, h: (b, h)
    at = lambda col: (lambda b, h: (b, col + h))
    shared = lambda b, h: (b, 0)
    blockspec = lambda m: pl.BlockSpec((lp, LANES), m)
    return pl.pallas_call(
        functools.partial(_attn_kernel, qe_head_lanes=qe_head_lanes),
        out_shape=jax.ShapeDtypeStruct((tp, n_heads * LANES), jnp.bfloat16),
        grid=(batch, n_heads),
        in_specs=[blockspec(at(qm_col)), blockspec(shared if qe_head_lanes else head),
                  blockspec(at(km_col)), blockspec(shared), blockspec(at(v_col))],
        out_specs=blockspec(head),
        scratch_shapes=[pltpu.VMEM((lp, 2 * LANES), jnp.bfloat16),
                        pltpu.VMEM((lp, 2 * LANES), jnp.bfloat16)],
        compiler_params=_cparams(("parallel", "parallel")),
        name="attention",
    )(qm, qe, km, ke, v)


def _gated_conv_rows(b_ref, c_ref, x_ref, w_ref, tail_ref, rows_per_batch):
    i = pl.program_id(0)
    tm = b_ref.shape[0]

    @pl.when(i == 0)
    def _():
        tail_ref[...] = jnp.zeros_like(tail_ref)

    u = c_ref[...].astype(jnp.float32) * x_ref[...].astype(jnp.float32)
    rows = lax.broadcasted_iota(jnp.int32, u.shape, 0)
    start = (rows_per_batch - (i * tm) % rows_per_batch) % rows_per_batch
    prev1 = tail_ref[7:8, :]
    prev2 = tail_ref[6:7, :]
    u1 = jnp.where(rows == 0, prev1, pltpu.roll(u, 1, 0))
    u1 = jnp.where(rows == start, 0.0, u1)
    u2 = jnp.where(rows == 0, prev2, jnp.where(rows == 1, prev1, pltpu.roll(u, 2, 0)))
    u2 = jnp.where((rows == start) | (rows == start + 1), 0.0, u2)
    tail_ref[...] = u[tm - 8:tm, :]
    w = w_ref[...]
    return b_ref[...].astype(jnp.float32) * (w[2:3, :] * u + w[1:2, :] * u1 + w[0:1, :] * u2)


def _project_residual(a, w_ref, h_ref, gpost_ref, gnext_ref, ho_ref, hn_ref):
    y = jnp.dot(a, w_ref[...], preferred_element_type=jnp.float32)
    h_new = h_ref[...] + _rms(y, gpost_ref[...])
    ho_ref[...] = h_new
    if hn_ref is not None:
        hn_ref[...] = _rms(h_new, gnext_ref[...]).astype(hn_ref.dtype)


def _merge_out_kernel(oa_ref, cb_ref, cc_ref, cx_ref, oc_ref, gate_ref, cw_ref, wb_ref, wo_ref, h_ref,
                      gpost_ref, gnext_ref, ho_ref, hn_ref, tail_ref, *, rows_per_batch):
    o_b = _gated_conv_rows(cb_ref, cc_ref, cx_ref, cw_ref, tail_ref, rows_per_batch).astype(jnp.bfloat16)
    merged = None
    for n, o in enumerate((oa_ref[...], o_b, oc_ref[...])):
        y = jnp.dot(o, wb_ref[n], preferred_element_type=jnp.float32)
        term = gate_ref[:, n * D_MODEL:(n + 1) * D_MODEL].astype(jnp.float32) * y
        merged = term if merged is None else merged + term
    _project_residual(merged.astype(jnp.bfloat16), wo_ref, h_ref, gpost_ref, gnext_ref, ho_ref, hn_ref)


def _merge_out(o_a, convf, o_c, gates, conv_w, wb, wo, h, gpost, gnext, l, lp, tm):
    tp = h.shape[0]
    assert math.gcd(tm, lp) > 1
    row = lambda i: (i, 0)
    single = pl.Buffered(1)
    return pl.pallas_call(
        functools.partial(_merge_out_kernel, rows_per_batch=lp),
        out_shape=(jax.ShapeDtypeStruct((tp, D_MODEL), jnp.float32),
                   jax.ShapeDtypeStruct((tp, D_MODEL), jnp.bfloat16)),
        grid=(tp // tm,),
        in_specs=[pl.BlockSpec((tm, BRANCH_WIDTH), row),
                  pl.BlockSpec((tm, IN_TILE), lambda i: (i, 0)),
                  pl.BlockSpec((tm, IN_TILE), lambda i: (i, 1)),
                  pl.BlockSpec((tm, IN_TILE), lambda i: (i, 2)),
                  pl.BlockSpec((tm, BRANCH_WIDTH), row),
                  pl.BlockSpec((tm, N_BRANCH * D_MODEL), row),
                  _layer_spec((8, CONV_WIDTH), l),
                  pl.BlockSpec((None, N_BRANCH, BRANCH_WIDTH, D_MODEL), lambda i: (l, 0, 0, 0),
                               pipeline_mode=single),
                  pl.BlockSpec((None, D_MODEL, D_MODEL), lambda i: (l, 0, 0), pipeline_mode=single),
                  pl.BlockSpec((tm, D_MODEL), row),
                  _layer_spec((1, D_MODEL), l),
                  _layer_spec((1, D_MODEL), l)],
        out_specs=(pl.BlockSpec((tm, D_MODEL), row), pl.BlockSpec((tm, D_MODEL), row)),
        scratch_shapes=[pltpu.VMEM((8, CONV_WIDTH), jnp.float32)],
        compiler_params=_cparams(("arbitrary",)),
        name="merge_out",
    )(o_a, convf, convf, convf, o_c, gates, conv_w, wb, wo, h, gpost, gnext)


def _ffn_in_kernel(x_ref, wg_ref, wu_ref, o_ref, wg_scr, wu_scr):
    @pl.when(pl.program_id(1) == 0)
    def _():
        for r in range(0, D_MODEL, CAST_ROWS):
            wg_scr[r:r + CAST_ROWS, :] = wg_ref[r:r + CAST_ROWS, :].astype(jnp.bfloat16)
            wu_scr[r:r + CAST_ROWS, :] = wu_ref[r:r + CAST_ROWS, :].astype(jnp.bfloat16)

    x = x_ref[...]
    for c in range(0, FF_TILE, MXU_N):
        g = jnp.dot(x, wg_scr[:, c:c + MXU_N], preferred_element_type=jnp.float32)
        u = jnp.dot(x, wu_scr[:, c:c + MXU_N], preferred_element_type=jnp.float32)
        o_ref[:, c:c + MXU_N] = (g * _sigmoid(g) * u).astype(o_ref.dtype)


def _ffn_in(xn, w, l, tm):
    tp = xn.shape[0]
    n_tiles = D_FF // FF_TILE
    return pl.pallas_call(
        _ffn_in_kernel,
        out_shape=jax.ShapeDtypeStruct((tp, D_FF), jnp.bfloat16),
        grid=(n_tiles, tp // tm),
        in_specs=[pl.BlockSpec((tm, D_MODEL), lambda j, i: (i, 0)),
                  pl.BlockSpec((None, D_MODEL, FF_TILE), lambda j, i: (l, 0, j)),
                  pl.BlockSpec((None, D_MODEL, FF_TILE), lambda j, i: (l, 0, j + n_tiles))],
        out_specs=pl.BlockSpec((tm, FF_TILE), lambda j, i: (i, j)),
        scratch_shapes=[pltpu.VMEM((D_MODEL, FF_TILE), jnp.bfloat16),
                        pltpu.VMEM((D_MODEL, FF_TILE), jnp.bfloat16)],
        compiler_params=_cparams(("arbitrary", "arbitrary")),
        name="ffn_in",
    )(xn, w, w)


def _ffn_out_kernel(a_ref, w_ref, h_ref, gpost_ref, gnext_ref, ho_ref, hn_ref):
    _project_residual(a_ref[...], w_ref, h_ref, gpost_ref, gnext_ref, ho_ref, hn_ref)


def _ffn_out_last_kernel(a_ref, w_ref, h_ref, gpost_ref, ho_ref):
    _project_residual(a_ref[...], w_ref, h_ref, gpost_ref, None, ho_ref, None)


def _ffn_out(act, w, h, gpost, gnext, l, tm):
    tp = h.shape[0]
    row = lambda i: (i, 0)
    in_specs = [pl.BlockSpec((tm, D_FF), row),
                pl.BlockSpec((None, D_FF, D_MODEL), lambda i: (l, 0, 0), pipeline_mode=pl.Buffered(1)),
                pl.BlockSpec((tm, D_MODEL), row),
                _layer_spec((1, D_MODEL), l)]
    h_shape = jax.ShapeDtypeStruct((tp, D_MODEL), jnp.float32)
    h_spec = pl.BlockSpec((tm, D_MODEL), row)
    if gnext is None:
        return pl.pallas_call(
            _ffn_out_last_kernel, out_shape=h_shape, grid=(tp // tm,), in_specs=in_specs,
            out_specs=h_spec, compiler_params=_cparams(("parallel",)), name="ffn_out_last",
        )(act, w, h, gpost), None
    return pl.pallas_call(
        _ffn_out_kernel,
        out_shape=(h_shape, jax.ShapeDtypeStruct((tp, D_MODEL), jnp.bfloat16)),
        grid=(tp // tm,),
        in_specs=in_specs + [_layer_spec((1, D_MODEL), l + 1)],
        out_specs=(h_spec, pl.BlockSpec((tm, D_MODEL), row)),
        compiler_params=_cparams(("parallel",)),
        name="ffn_out",
    )(act, w, h, gpost, gnext)


def _prep_small_weights(w_uq, w_ukv):
    bf = jnp.bfloat16
    depth = w_uq.shape[0]
    half = MLA_ROPE // 2
    wq = w_uq.reshape(depth, MLA_Q_RANK, MLA_HEADS, MLA_NOPE + MLA_ROPE)
    pe = wq[..., MLA_NOPE:]
    rot = jnp.concatenate([-pe[..., half:], pe[..., :half]], axis=-1)
    wq = jnp.concatenate([wq[..., :MLA_NOPE], pe, rot], axis=-1)
    wq = wq.reshape(depth, MLA_Q_RANK, MLA_HEADS * 2 * LANES).astype(bf)
    wkv = w_ukv.reshape(depth, MLA_KV_RANK, MLA_HEADS, MLA_NOPE + MLA_V)
    wkv = jnp.concatenate([wkv[..., :MLA_NOPE].reshape(depth, MLA_KV_RANK, -1),
                           wkv[..., MLA_NOPE:].reshape(depth, MLA_KV_RANK, -1)], axis=-1).astype(bf)
    return wq, wkv


def _rope_tables(lp):
    inv_freq = 1.0 / (ROPE_THETA ** (jnp.arange(0, MLA_ROPE, 2, dtype=jnp.float32) / MLA_ROPE))
    pos = jnp.arange(lp, dtype=jnp.float32)
    ang = pos[:, None] * inv_freq[None, :]
    zeros = jnp.zeros((lp, LANES - MLA_ROPE), jnp.float32)
    cos_t = jnp.concatenate([jnp.cos(ang), jnp.cos(ang), zeros], axis=1)
    sin_t = jnp.concatenate([jnp.sin(ang), jnp.sin(ang), zeros], axis=1)
    return cos_t, sin_t


def _decay_selectors(batch):
    src = jnp.arange(3 * LANES)
    dst = jnp.arange(FOX_HEADS * DECAY_LANES)
    part, lane_src = src // LANES, src % LANES
    head_dst, lane_dst = dst // DECAY_LANES, dst % DECAY_LANES
    b = jnp.arange(batch)[:, None, None]
    same_head = lane_src[None, :, None] == b * FOX_HEADS + head_dst[None, None, :]
    selq = (same_head & (lane_dst[None, None, :] == part[None, :, None])).astype(jnp.bfloat16)
    selk = -(same_head & (lane_dst[None, None, :] == part[None, :, None] + 3)).astype(jnp.bfloat16)
    return selq, selk


def kernel(x, meta, w_in, b_forget, g_q_lat, g_kv_lat, w_uq, w_ukv, conv_w, w_branch, w_out,
           w_ffn_in, w_ffn_out, g_mix_pre, g_mix_post, g_ffn_pre, g_ffn_post):
    batch, seq, _ = x.shape
    depth = w_in.shape[0]
    lp = N_META + seq
    tp = batch * lp
    tm_huge = _row_tile(tp, 2100)
    tm_big = _row_tile(tp, 1400)
    tm_mid = _row_tile(lp, 700)
    tm_small = _row_tile(tp, 288)

    bf = jnp.bfloat16
    wq, wkv = _prep_small_weights(w_uq, w_ukv)
    wb, wo, wfo = w_branch.astype(bf), w_out.astype(bf), w_ffn_out.astype(bf)
    cos_t, sin_t = _rope_tables(lp)
    selq, selk = _decay_selectors(batch)
    tri = (jnp.arange(tm_mid)[:, None] >= jnp.arange(tm_mid)[None, :]).astype(bf)
    bias = jnp.tile(b_forget.astype(jnp.float32), (1, LANES // FOX_HEADS))[:, None, :]
    w_in_t = jnp.swapaxes(w_in, 1, 2)
    conv_w8 = jnp.pad(conv_w.astype(jnp.float32), ((0, 0), (0, 8 - CONV_K), (0, 0)))
    row2 = lambda g: g.astype(jnp.float32)[:, None, :]
    g_q, g_kv = row2(g_q_lat), row2(g_kv_lat)
    g_mpre, g_mpost, g_fpre, g_fpost = row2(g_mix_pre), row2(g_mix_post), row2(g_ffn_pre), row2(g_ffn_post)
    srow = jnp.ones((1, N_CONVF_TILES * IN_TILE), jnp.float32)
    srow = srow.at[:, 3 * CONV_WIDTH:3 * CONV_WIDTH + FOX_HEADS * FOX_HEAD_DIM].set(FOX_QSCALE)
    lanes_per_tile = IN_TILE // LANES
    fq_col, fk_col, fv_col = (3 * lanes_per_tile, 4 * lanes_per_tile, 5 * lanes_per_tile)

    h = jnp.concatenate([jnp.broadcast_to(meta[None].astype(x.dtype), (batch, N_META, D_MODEL)), x], axis=1)
    h = h.reshape(tp, D_MODEL)
    hn = _prenorm(h, g_mpre, 0, tm_mid)

    for l in range(depth):
        gates = _wproj(hn, w_in_t, srow, l, COL_GATE, N_GATE_TILES, True, tm_big)
        convf = _wproj(hn, w_in_t, srow, l, COL_CONV, N_CONVF_TILES, False, tm_big)
        lat, small = _latproj(hn, w_in_t, l, tm_big)
        qm, qe, km, ke, v_a = _mla_latent(lat, small, g_q, g_kv, wq, wkv, cos_t, sin_t, l, lp, tm_mid)
        o_a = _attention(qm, qe, km, ke, v_a, batch, lp, MLA_HEADS)
        aq, ak = _fox_decay(small, bias, tri, selq, selk, l, batch, lp)
        o_c = _attention(convf, aq, convf, ak, convf, batch, lp, FOX_HEADS, qe_head_lanes=DECAY_LANES,
                         qm_col=fq_col, km_col=fk_col, v_col=fv_col)
        h, hn = _merge_out(o_a, convf, o_c, gates, conv_w8, wb, wo, h, g_mpost, g_fpre, l, lp, tm_small)
        act = _ffn_in(hn, w_ffn_in, l, tm_huge)
        gnext = g_mpre if l + 1 < depth else None
        h, hn = _ffn_out(act, wfo, h, g_fpost, gnext, l, tm_small)

    return h.reshape(batch, lp, D_MODEL)[:, N_META:]
```

```python
import functools
import math

import jax
import jax.numpy as jnp
from jax import lax
from jax.experimental import pallas as pl
from jax.experimental.pallas import tpu as pltpu

D_MODEL = 2048
N_META = 16
EPS = 1e-6
NEG_INF = -1e30
ROPE_THETA = 10000.0
MLA_HEADS = 8
MLA_Q_RANK = 512
MLA_KV_RANK = 512
MLA_NOPE = 128
MLA_ROPE = 64
MLA_V = 128
CONV_WIDTH = 1024
CONV_K = 3
FOX_HEADS = 8
FOX_HEAD_DIM = 128
N_BRANCH = 3
BRANCH_WIDTH = 1024
D_FF = 5632

LOG2E = 1.4426950408889634
MLA_QSCALE = (MLA_NOPE + MLA_ROPE) ** -0.5 * LOG2E
FOX_QSCALE = FOX_HEAD_DIM ** -0.5 * LOG2E

LANES = 128
MXU_N = 256
VMEM_LIMIT_BYTES = 56 * 1024 * 1024

BF16_ROWS = 16

IN_TILE = 1024
COL_KPE = MLA_Q_RANK + MLA_KV_RANK
COL_CONV = COL_KPE + MLA_ROPE
N_CONVF_TILES = (3 * CONV_WIDTH + 3 * FOX_HEADS * FOX_HEAD_DIM) // IN_TILE
COL_FLOG = COL_CONV + N_CONVF_TILES * IN_TILE
COL_GATE = COL_FLOG + FOX_HEADS
N_GATE_TILES = N_BRANCH * D_MODEL // IN_TILE
SMALL_W = 2 * LANES
CAST_ROWS = 256
FF_TILE = 512


def _row_tile(rows, target):
    return max(t for t in range(BF16_ROWS, target + 1, BF16_ROWS) if rows % t == 0)


def _cparams(semantics):
    return pltpu.CompilerParams(dimension_semantics=semantics, vmem_limit_bytes=VMEM_LIMIT_BYTES)


def _rms(x, g):
    ms = jnp.mean(x * x, axis=-1, keepdims=True)
    return x * lax.rsqrt(ms + EPS) * g


def _sigmoid(x):
    return 1.0 / (1.0 + jnp.exp(-x))


def _layer_spec(shape, l):
    zeros = (0,) * len(shape)
    return pl.BlockSpec((None,) + tuple(shape), lambda *_: (l,) + zeros)


def _embed_kernel(x_ref, meta_ref, g_ref, h_ref, hn_ref):
    tm = h_ref.shape[0]

    def emit(h):
        h_ref[...] = h
        hn_ref[...] = _rms(h, g_ref[...]).astype(hn_ref.dtype)

    @pl.when(pl.program_id(1) == 0)
    def _():
        emit(jnp.concatenate([meta_ref[...], x_ref[0:tm - N_META, :]], axis=0))

    @pl.when(pl.program_id(1) > 0)
    def _():
        emit(x_ref[...])


def _embed(x, meta, g, l, tm):
    batch, seq, _ = x.shape
    lp = N_META + seq
    per_batch = lp // tm
    assert per_batch * tm == lp and tm > N_META and N_META % 8 == 0
    row = lambda b, t: (b * per_batch + t, 0)
    x_row = lambda b, t: pl.multiple_of(b * seq + jnp.maximum(t * tm - N_META, 0), 8)
    return pl.pallas_call(
        _embed_kernel,
        out_shape=(jax.ShapeDtypeStruct((batch * lp, D_MODEL), jnp.float32),
                   jax.ShapeDtypeStruct((batch * lp, D_MODEL), jnp.bfloat16)),
        grid=(batch, per_batch),
        in_specs=[pl.BlockSpec((pl.Element(tm), pl.Element(D_MODEL)), lambda b, t: (x_row(b, t), 0)),
                  pl.BlockSpec((N_META, D_MODEL), lambda b, t: (0, 0)),
                  _layer_spec((1, D_MODEL), l)],
        out_specs=(pl.BlockSpec((tm, D_MODEL), row), pl.BlockSpec((tm, D_MODEL), row)),
        compiler_params=_cparams(("parallel", "parallel")),
        name="embed",
    )(x.reshape(batch * seq, D_MODEL), meta.astype(x.dtype), g)


_NT = (((1,), (1,)), ((), ()))


def _wproj_kernel(x_ref, w_ref, srow_ref, o_ref, w_scr, *, sigmoid):
    @pl.when(pl.program_id(1) == 0)
    def _():
        for r in range(0, IN_TILE, CAST_ROWS):
            w_scr[r:r + CAST_ROWS, :] = w_ref[0, r:r + CAST_ROWS, :].astype(jnp.bfloat16)

    acc = lax.dot_general(x_ref[...], w_scr[...], _NT, preferred_element_type=jnp.float32)
    y = _sigmoid(acc) if sigmoid else acc * srow_ref[...]
    o_ref[...] = y.astype(o_ref.dtype)


def _wproj(xn, w_in_t, srow, l, col0, n_tiles, sigmoid, tm):
    tp = xn.shape[0]
    assert col0 % 8 == 0
    return pl.pallas_call(
        functools.partial(_wproj_kernel, sigmoid=sigmoid),
        out_shape=jax.ShapeDtypeStruct((tp, n_tiles * IN_TILE), jnp.bfloat16),
        grid=(n_tiles, tp // tm),
        in_specs=[pl.BlockSpec((tm, D_MODEL), lambda j, i: (i, 0)),
                  pl.BlockSpec((pl.Element(1), pl.Element(IN_TILE), pl.Element(D_MODEL)),
                               lambda j, i: (l, pl.multiple_of(col0 + j * IN_TILE, 8), 0)),
                  pl.BlockSpec((1, IN_TILE), lambda j, i: (0, j))],
        out_specs=pl.BlockSpec((tm, IN_TILE), lambda j, i: (i, j)),
        scratch_shapes=[pltpu.VMEM((IN_TILE, D_MODEL), jnp.bfloat16)],
        compiler_params=_cparams(("arbitrary", "arbitrary")),
        name="wproj_gates" if sigmoid else "wproj_convf",
    )(xn, w_in_t, srow)


def _latproj_kernel(x_ref, wl_ref, wk_ref, wf_ref, lat_ref, small_ref, wl_scr, ws_scr):
    @pl.when(pl.program_id(0) == 0)
    def _():
        for r in range(0, IN_TILE, CAST_ROWS):
            wl_scr[r:r + CAST_ROWS, :] = wl_ref[r:r + CAST_ROWS, :].astype(jnp.bfloat16)
        kpe = wk_ref[...]
        half = MLA_ROPE // 2
        fill = jnp.zeros((SMALL_W - 2 * MLA_ROPE - FOX_HEADS, D_MODEL), jnp.float32)
        ws = jnp.concatenate([kpe, -kpe[half:], kpe[:half], wf_ref[...], fill], axis=0)
        ws_scr[...] = ws.astype(jnp.bfloat16)

    x = x_ref[...]
    lat_ref[...] = lax.dot_general(x, wl_scr[...], _NT, preferred_element_type=jnp.float32).astype(lat_ref.dtype)
    small_ref[...] = lax.dot_general(x, ws_scr[...], _NT, preferred_element_type=jnp.float32)


def _latproj(xn, w_in_t, l, tm):
    tp = xn.shape[0]
    return pl.pallas_call(
        _latproj_kernel,
        out_shape=(jax.ShapeDtypeStruct((tp, IN_TILE), jnp.bfloat16),
                   jax.ShapeDtypeStruct((tp, SMALL_W), jnp.float32)),
        grid=(tp // tm,),
        in_specs=[pl.BlockSpec((tm, D_MODEL), lambda i: (i, 0)),
                  pl.BlockSpec((None, IN_TILE, D_MODEL), lambda i: (l, 0, 0), pipeline_mode=pl.Buffered(1)),
                  pl.BlockSpec((None, MLA_ROPE, D_MODEL), lambda i: (l, COL_KPE // MLA_ROPE, 0)),
                  pl.BlockSpec((None, FOX_HEADS, D_MODEL), lambda i: (l, COL_FLOG // FOX_HEADS, 0))],
        out_specs=(pl.BlockSpec((tm, IN_TILE), lambda i: (i, 0)),
                   pl.BlockSpec((tm, SMALL_W), lambda i: (i, 0))),
        scratch_shapes=[pltpu.VMEM((IN_TILE, D_MODEL), jnp.bfloat16),
                        pltpu.VMEM((SMALL_W, D_MODEL), jnp.bfloat16)],
        compiler_params=_cparams(("arbitrary",)),
        name="latproj",
    )(xn, w_in_t, w_in_t, w_in_t)


def _rope_pair(v, cos_t, sin_t):
    return v * cos_t + pltpu.roll(v, 64, 1) * sin_t


def _mla_latent_kernel(lat_ref, small_ref, gq_ref, gkv_ref, wq_ref, wkv_ref, cos_ref, sin_ref,
                       qm_ref, qe_ref, km_ref, ke_ref, v_ref):
    lat = lat_ref[...].astype(jnp.float32)
    cos_t = cos_ref[...]
    sin_t = sin_ref[...]
    cq = _rms(lat[:, :MLA_Q_RANK], gq_ref[...]).astype(jnp.bfloat16)
    ckv = _rms(lat[:, MLA_Q_RANK:], gkv_ref[...]).astype(jnp.bfloat16)
    q = jnp.dot(cq, wq_ref[...], preferred_element_type=jnp.float32)
    for h in range(MLA_HEADS):
        base = h * 2 * LANES
        qm_ref[:, h * LANES:(h + 1) * LANES] = (q[:, base:base + LANES] * MLA_QSCALE).astype(qm_ref.dtype)
        roped = _rope_pair(q[:, base + LANES:base + 2 * LANES], cos_t, sin_t)
        qe_ref[:, h * LANES:(h + 1) * LANES] = (roped * MLA_QSCALE).astype(qe_ref.dtype)
    kv = jnp.dot(ckv, wkv_ref[...], preferred_element_type=jnp.float32)
    km_ref[...] = kv[:, :MLA_HEADS * MLA_NOPE].astype(km_ref.dtype)
    v_ref[...] = kv[:, MLA_HEADS * MLA_NOPE:].astype(v_ref.dtype)
    ke_ref[...] = _rope_pair(small_ref[...], cos_t, sin_t).astype(ke_ref.dtype)


def _mla_latent(lat, small, gq, gkv, wq, wkv, cos_t, sin_t, l, lp, tm):
    tp = lat.shape[0]
    per_batch = lp // tm
    hw = MLA_HEADS * LANES
    bf = jnp.bfloat16
    row = lambda i: (i, 0)
    return pl.pallas_call(
        _mla_latent_kernel,
        out_shape=(jax.ShapeDtypeStruct((tp, hw), bf), jax.ShapeDtypeStruct((tp, hw), bf),
                   jax.ShapeDtypeStruct((tp, hw), bf), jax.ShapeDtypeStruct((tp, LANES), bf),
                   jax.ShapeDtypeStruct((tp, hw), bf)),
        grid=(tp // tm,),
        in_specs=[pl.BlockSpec((tm, IN_TILE), row),
                  pl.BlockSpec((tm, LANES), row),
                  _layer_spec((1, MLA_Q_RANK), l),
                  _layer_spec((1, MLA_KV_RANK), l),
                  _layer_spec((MLA_Q_RANK, 2 * hw), l),
                  _layer_spec((MLA_KV_RANK, 2 * hw), l),
                  pl.BlockSpec((tm, LANES), lambda i: (i % per_batch, 0)),
                  pl.BlockSpec((tm, LANES), lambda i: (i % per_batch, 0))],
        out_specs=(pl.BlockSpec((tm, hw), row), pl.BlockSpec((tm, hw), row),
                   pl.BlockSpec((tm, hw), row), pl.BlockSpec((tm, LANES), row),
                   pl.BlockSpec((tm, hw), row)),
        compiler_params=_cparams(("parallel",)),
        name="mla_latent",
    )(lat, small, gq, gkv, wq, wkv, cos_t, sin_t)


def _split3(x):
    hi = x.astype(jnp.bfloat16)
    r1 = x - hi.astype(jnp.float32)
    mid = r1.astype(jnp.bfloat16)
    lo = (r1 - mid.astype(jnp.float32)).astype(jnp.bfloat16)
    return hi, mid, lo


def _fox_decay_kernel(fl_ref, bias_ref, tri_ref, selq_ref, selk_ref, aq_ref, ak_ref, carry_ref):
    blk = pl.program_id(0)

    @pl.when(blk == 0)
    def _():
        carry_ref[...] = jnp.zeros_like(carry_ref)

    tri = tri_ref[...]
    rows_per_blk = tri.shape[0]
    lane = lax.broadcasted_iota(jnp.int32, aq_ref.shape[1:], 1)
    z = fl_ref[0]
    for b in range(1, fl_ref.shape[0]):
        lane0 = b * FOX_HEADS
        z = jnp.where((lane >= lane0) & (lane < lane0 + FOX_HEADS), pltpu.roll(fl_ref[b], lane0, 1), z)
    z = z + bias_ref[...]
    log_f = (jnp.minimum(z, 0.0) - jnp.log1p(jnp.exp(-jnp.abs(z)))) * LOG2E
    c = carry_ref[0:1, :]
    for part in _split3(log_f):
        c = c + jnp.dot(tri, part, preferred_element_type=jnp.float32)
    carry_ref[...] = jnp.broadcast_to(c[rows_per_blk - 1:rows_per_blk, :], carry_ref.shape)
    parts = jnp.concatenate(_split3(c), axis=1)
    lane = lane % DECAY_LANES
    for b in range(fl_ref.shape[0]):
        aq = jnp.dot(parts, selq_ref[b], preferred_element_type=jnp.float32)
        ak = jnp.dot(parts, selk_ref[b], preferred_element_type=jnp.float32)
        aq_ref[b] = jnp.where((lane >= 3) & (lane < 6), 1.0, aq).astype(aq_ref.dtype)
        ak_ref[b] = jnp.where(lane < 3, 1.0, ak).astype(ak_ref.dtype)


def _fox_decay(small, bias, tri, selq, selk, l, batch, lp):
    tp = small.shape[0]
    blk_rows = tri.shape[0]
    nblk = lp // blk_rows
    hw = FOX_HEADS * DECAY_LANES
    assert batch * FOX_HEADS <= LANES and selq.shape == (batch, 3 * LANES, hw)
    const = lambda i: (0, 0)
    const3 = lambda i: (0, 0, 0)
    out = jax.ShapeDtypeStruct((batch, lp, hw), jnp.bfloat16)
    aq, ak = pl.pallas_call(
        _fox_decay_kernel,
        out_shape=(out, out),
        grid=(nblk,),
        in_specs=[pl.BlockSpec((batch, blk_rows, LANES), lambda i: (0, i, 1)),
                  _layer_spec((1, LANES), l),
                  pl.BlockSpec((blk_rows, blk_rows), const),
                  pl.BlockSpec((batch, 3 * LANES, hw), const3),
                  pl.BlockSpec((batch, 3 * LANES, hw), const3)],
        out_specs=(pl.BlockSpec((batch, blk_rows, hw), lambda i: (0, i, 0)),
                   pl.BlockSpec((batch, blk_rows, hw), lambda i: (0, i, 0))),
        scratch_shapes=[pltpu.VMEM((8, LANES), jnp.float32)],
        compiler_params=_cparams(("arbitrary",)),
        name="fox_decay",
    )(small.reshape(batch, lp, SMALL_W), bias, tri, selq, selk)
    return aq.reshape(tp, hw), ak.reshape(tp, hw)


Q_TILE = 256
DECAY_LANES = LANES // FOX_HEADS


def _attn_scores(qc_ref, kc_ref, r0, tq):
    q = qc_ref[r0:r0 + tq, :]

    def scores(k0, k1):
        return lax.dot_general(q, kc_ref[k0:k1, :], _NT, preferred_element_type=jnp.float32)

    pieces = []
    if r0 > 0:
        pieces.append((scores(0, N_META), 0, N_META))
        if r0 > N_META:
            pieces.append((scores(N_META, r0), N_META, r0))
    causal = (lax.broadcasted_iota(jnp.int32, (tq, tq), 1)
              <= lax.broadcasted_iota(jnp.int32, (tq, tq), 0))
    pieces.append((jnp.where(causal, scores(r0, r0 + tq), NEG_INF), r0, r0 + tq))
    return pieces


def _attn_softmax_pv(pieces, v_ref):
    m = None
    for s, _, _ in pieces:
        pm = jnp.max(s, axis=1, keepdims=True)
        m = pm if m is None else jnp.maximum(m, pm)
    l = None
    acc = None
    for s, k0, k1 in pieces:
        p = jnp.exp2(s - m)
        ps = jnp.sum(p, axis=1, keepdims=True)
        pv = jnp.dot(p.astype(jnp.bfloat16), v_ref[k0:k1, :], preferred_element_type=jnp.float32)
        l = ps if l is None else l + ps
        acc = pv if acc is None else acc + pv
    return acc / l


def _attn_kernel(qm_ref, qe_ref, km_ref, ke_ref, v_ref, o_ref, qc_ref, kc_ref, *, qe_head_lanes):
    qe = qe_ref[...]
    if qe_head_lanes:
        lane0 = pl.program_id(1) * qe_head_lanes
        lane = lax.broadcasted_iota(jnp.int32, qe.shape, 1)
        qe = jnp.where((lane >= lane0) & (lane < lane0 + qe_head_lanes), qe, jnp.zeros_like(qe))
    qc_ref[:, :LANES] = qm_ref[...]
    qc_ref[:, LANES:] = qe
    kc_ref[:, :LANES] = km_ref[...]
    kc_ref[:, LANES:] = ke_ref[...]
    lp = o_ref.shape[0]
    tiles = [(0, N_META)] + [(r0, Q_TILE) for r0 in range(N_META, lp, Q_TILE)]
    pieces = _attn_scores(qc_ref, kc_ref, *tiles[0])
    for t, (r0, tq) in enumerate(tiles):
        nxt = _attn_scores(qc_ref, kc_ref, *tiles[t + 1]) if t + 1 < len(tiles) else None
        o_ref[r0:r0 + tq, :] = _attn_softmax_pv(pieces, v_ref).astype(o_ref.dtype)
        pieces = nxt


def _attention(qm, qe, km, ke, v, batch, lp, n_heads, qe_head_lanes=0, qm_col=0, km_col=0, v_col=0):
    assert (lp - N_META) % Q_TILE == 0
    tp = qm.shape[0]
    head = lambda b, h: (b, h)
    at = lambda col: (lambda b, h: (b, col + h))
    shared = lambda b, h: (b, 0)
    blockspec = lambda m: pl.BlockSpec((lp, LANES), m)
    return pl.pallas_call(
        functools.partial(_attn_kernel, qe_head_lanes=qe_head_lanes),
        out_shape=jax.ShapeDtypeStruct((tp, n_heads * LANES), jnp.bfloat16),
        grid=(batch, n_heads),
        in_specs=[blockspec(at(qm_col)), blockspec(shared if qe_head_lanes else head),
                  blockspec(at(km_col)), blockspec(shared), blockspec(at(v_col))],
        out_specs=blockspec(head),
        scratch_shapes=[pltpu.VMEM((lp, 2 * LANES), jnp.bfloat16),
                        pltpu.VMEM((lp, 2 * LANES), jnp.bfloat16)],
        compiler_params=_cparams(("parallel", "parallel")),
        name="attention",
    )(qm, qe, km, ke, v)


def _gated_conv_rows(b_ref, c_ref, x_ref, w_ref, tail_ref, rows_per_batch):
    i = pl.program_id(0)
    tm = b_ref.shape[0]

    @pl.when(i == 0)
    def _():
        tail_ref[...] = jnp.zeros_like(tail_ref)

    u = c_ref[...].astype(jnp.float32) * x_ref[...].astype(jnp.float32)
    rows = lax.broadcasted_iota(jnp.int32, u.shape, 0)
    start = (rows_per_batch - (i * tm) % rows_per_batch) % rows_per_batch
    prev1 = tail_ref[7:8, :]
    prev2 = tail_ref[6:7, :]
    u1 = jnp.where(rows == 0, prev1, pltpu.roll(u, 1, 0))
    u1 = jnp.where(rows == start, 0.0, u1)
    u2 = jnp.where(rows == 0, prev2, jnp.where(rows == 1, prev1, pltpu.roll(u, 2, 0)))
    u2 = jnp.where((rows == start) | (rows == start + 1), 0.0, u2)
    tail_ref[...] = u[tm - 8:tm, :]
    w = w_ref[...]
    return b_ref[...].astype(jnp.float32) * (w[2:3, :] * u + w[1:2, :] * u1 + w[0:1, :] * u2)


def _project_residual(a, w_ref, h_ref, gpost_ref, gnext_ref, ho_ref, hn_ref):
    y = jnp.dot(a, w_ref[...], preferred_element_type=jnp.float32)
    h_new = h_ref[...] + _rms(y, gpost_ref[...])
    ho_ref[...] = h_new
    if hn_ref is not None:
        hn_ref[...] = _rms(h_new, gnext_ref[...]).astype(hn_ref.dtype)


def _merge_out_kernel(oa_ref, cb_ref, cc_ref, cx_ref, oc_ref, gate_ref, cw_ref, wb_ref, wo_ref, h_ref,
                      gpost_ref, gnext_ref, ho_ref, hn_ref, tail_ref, *, rows_per_batch):
    o_b = _gated_conv_rows(cb_ref, cc_ref, cx_ref, cw_ref, tail_ref, rows_per_batch).astype(jnp.bfloat16)
    merged = None
    for n, o in enumerate((oa_ref[...], o_b, oc_ref[...])):
        y = jnp.dot(o, wb_ref[n], preferred_element_type=jnp.float32)
        term = gate_ref[:, n * D_MODEL:(n + 1) * D_MODEL].astype(jnp.float32) * y
        merged = term if merged is None else merged + term
    _project_residual(merged.astype(jnp.bfloat16), wo_ref, h_ref, gpost_ref, gnext_ref, ho_ref, hn_ref)


def _merge_out(o_a, convf, o_c, gates, conv_w, wb, wo, h, gpost, gnext, l, lp, tm):
    tp = h.shape[0]
    assert math.gcd(tm, lp) > 1
    row = lambda i: (i, 0)
    single = pl.Buffered(1)
    return pl.pallas_call(
        functools.partial(_merge_out_kernel, rows_per_batch=lp),
        out_shape=(jax.ShapeDtypeStruct((tp, D_MODEL), jnp.float32),
                   jax.ShapeDtypeStruct((tp, D_MODEL), jnp.bfloat16)),
        grid=(tp // tm,),
        in_specs=[pl.BlockSpec((tm, BRANCH_WIDTH), row),
                  pl.BlockSpec((tm, IN_TILE), lambda i: (i, 0)),
                  pl.BlockSpec((tm, IN_TILE), lambda i: (i, 1)),
                  pl.BlockSpec((tm, IN_TILE), lambda i: (i, 2)),
                  pl.BlockSpec((tm, BRANCH_WIDTH), row),
                  pl.BlockSpec((tm, N_BRANCH * D_MODEL), row),
                  _layer_spec((8, CONV_WIDTH), l),
                  pl.BlockSpec((None, N_BRANCH, BRANCH_WIDTH, D_MODEL), lambda i: (l, 0, 0, 0),
                               pipeline_mode=single),
                  pl.BlockSpec((None, D_MODEL, D_MODEL), lambda i: (l, 0, 0), pipeline_mode=single),
                  pl.BlockSpec((tm, D_MODEL), row),
                  _layer_spec((1, D_MODEL), l),
                  _layer_spec((1, D_MODEL), l)],
        out_specs=(pl.BlockSpec((tm, D_MODEL), row), pl.BlockSpec((tm, D_MODEL), row)),
        scratch_shapes=[pltpu.VMEM((8, CONV_WIDTH), jnp.float32)],
        compiler_params=_cparams(("arbitrary",)),
        name="merge_out",
    )(o_a, convf, convf, convf, o_c, gates, conv_w, wb, wo, h, gpost, gnext)


def _ffn_in_kernel(x_ref, wg_ref, wu_ref, o_ref, wg_scr, wu_scr):
    @pl.when(pl.program_id(1) == 0)
    def _():
        for r in range(0, D_MODEL, CAST_ROWS):
            wg_scr[r:r + CAST_ROWS, :] = wg_ref[r:r + CAST_ROWS, :].astype(jnp.bfloat16)
            wu_scr[r:r + CAST_ROWS, :] = wu_ref[r:r + CAST_ROWS, :].astype(jnp.bfloat16)

    x = x_ref[...]
    for c in range(0, FF_TILE, MXU_N):
        g = jnp.dot(x, wg_scr[:, c:c + MXU_N], preferred_element_type=jnp.float32)
        u = jnp.dot(x, wu_scr[:, c:c + MXU_N], preferred_element_type=jnp.float32)
        o_ref[:, c:c + MXU_N] = (g * _sigmoid(g) * u).astype(o_ref.dtype)


def _ffn_in(xn, w, l, tm):
    tp = xn.shape[0]
    n_tiles = D_FF // FF_TILE
    return pl.pallas_call(
        _ffn_in_kernel,
        out_shape=jax.ShapeDtypeStruct((tp, D_FF), jnp.bfloat16),
        grid=(n_tiles, tp // tm),
        in_specs=[pl.BlockSpec((tm, D_MODEL), lambda j, i: (i, 0)),
                  pl.BlockSpec((None, D_MODEL, FF_TILE), lambda j, i: (l, 0, j)),
                  pl.BlockSpec((None, D_MODEL, FF_TILE), lambda j, i: (l, 0, j + n_tiles))],
        out_specs=pl.BlockSpec((tm, FF_TILE), lambda j, i: (i, j)),
        scratch_shapes=[pltpu.VMEM((D_MODEL, FF_TILE), jnp.bfloat16),
                        pltpu.VMEM((D_MODEL, FF_TILE), jnp.bfloat16)],
        compiler_params=_cparams(("arbitrary", "arbitrary")),
        name="ffn_in",
    )(xn, w, w)


def _ffn_out_kernel(a_ref, w_ref, h_ref, gpost_ref, gnext_ref, ho_ref, hn_ref):
    _project_residual(a_ref[...], w_ref, h_ref, gpost_ref, gnext_ref, ho_ref, hn_ref)


def _ffn_out_last_kernel(a_ref, w_ref, h_ref, gpost_ref, ho_ref):
    _project_residual(a_ref[...], w_ref, h_ref, gpost_ref, None, ho_ref, None)


def _ffn_out(act, w, h, gpost, gnext, l, tm):
    tp = h.shape[0]
    row = lambda i: (i, 0)
    in_specs = [pl.BlockSpec((tm, D_FF), row),
                pl.BlockSpec((None, D_FF, D_MODEL), lambda i: (l, 0, 0), pipeline_mode=pl.Buffered(1)),
                pl.BlockSpec((tm, D_MODEL), row),
                _layer_spec((1, D_MODEL), l)]
    h_shape = jax.ShapeDtypeStruct((tp, D_MODEL), jnp.float32)
    h_spec = pl.BlockSpec((tm, D_MODEL), row)
    if gnext is None:
        return pl.pallas_call(
            _ffn_out_last_kernel, out_shape=h_shape, grid=(tp // tm,), in_specs=in_specs,
            out_specs=h_spec, compiler_params=_cparams(("parallel",)), name="ffn_out_last",
        )(act, w, h, gpost), None
    return pl.pallas_call(
        _ffn_out_kernel,
        out_shape=(h_shape, jax.ShapeDtypeStruct((tp, D_MODEL), jnp.bfloat16)),
        grid=(tp // tm,),
        in_specs=in_specs + [_layer_spec((1, D_MODEL), l + 1)],
        out_specs=(h_spec, pl.BlockSpec((tm, D_MODEL), row)),
        compiler_params=_cparams(("parallel",)),
        name="ffn_out",
    )(act, w, h, gpost, gnext)


def _prep_small_weights(w_uq, w_ukv):
    bf = jnp.bfloat16
    depth = w_uq.shape[0]
    half = MLA_ROPE // 2
    wq = w_uq.reshape(depth, MLA_Q_RANK, MLA_HEADS, MLA_NOPE + MLA_ROPE)
    pe = wq[..., MLA_NOPE:]
    rot = jnp.concatenate([-pe[..., half:], pe[..., :half]], axis=-1)
    wq = jnp.concatenate([wq[..., :MLA_NOPE], pe, rot], axis=-1)
    wq = wq.reshape(depth, MLA_Q_RANK, MLA_HEADS * 2 * LANES).astype(bf)
    wkv = w_ukv.reshape(depth, MLA_KV_RANK, MLA_HEADS, MLA_NOPE + MLA_V)
    wkv = jnp.concatenate([wkv[..., :MLA_NOPE].reshape(depth, MLA_KV_RANK, -1),
                           wkv[..., MLA_NOPE:].reshape(depth, MLA_KV_RANK, -1)], axis=-1).astype(bf)
    return wq, wkv


def _rope_tables(lp):
    inv_freq = 1.0 / (ROPE_THETA ** (jnp.arange(0, MLA_ROPE, 2, dtype=jnp.float32) / MLA_ROPE))
    pos = jnp.arange(lp, dtype=jnp.float32)
    ang = pos[:, None] * inv_freq[None, :]
    zeros = jnp.zeros((lp, LANES - MLA_ROPE), jnp.float32)
    cos_t = jnp.concatenate([jnp.cos(ang), jnp.cos(ang), zeros], axis=1)
    sin_t = jnp.concatenate([jnp.sin(ang), jnp.sin(ang), zeros], axis=1)
    return cos_t, sin_t


def _decay_selectors(batch):
    src = jnp.arange(3 * LANES)
    dst = jnp.arange(FOX_HEADS * DECAY_LANES)
    part, lane_src = src // LANES, src % LANES
    head_dst, lane_dst = dst // DECAY_LANES, dst % DECAY_LANES
    b = jnp.arange(batch)[:, None, None]
    same_head = lane_src[None, :, None] == b * FOX_HEADS + head_dst[None, None, :]
    selq = (same_head & (lane_dst[None, None, :] == part[None, :, None])).astype(jnp.bfloat16)
    selk = -(same_head & (lane_dst[None, None, :] == part[None, :, None] + 3)).astype(jnp.bfloat16)
    return selq, selk


def kernel(x, meta, w_in, b_forget, g_q_lat, g_kv_lat, w_uq, w_ukv, conv_w, w_branch, w_out,
           w_ffn_in, w_ffn_out, g_mix_pre, g_mix_post, g_ffn_pre, g_ffn_post):
    batch, seq, _ = x.shape
    depth = w_in.shape[0]
    lp = N_META + seq
    tp = batch * lp
    tm_huge = _row_tile(tp, 2100)
    tm_big = _row_tile(tp, 1400)
    tm_mid = _row_tile(lp, 700)
    tm_small = _row_tile(tp, 288)

    bf = jnp.bfloat16
    wq, wkv = _prep_small_weights(w_uq, w_ukv)
    wb, wo, wfo = w_branch.astype(bf), w_out.astype(bf), w_ffn_out.astype(bf)
    cos_t, sin_t = _rope_tables(lp)
    selq, selk = _decay_selectors(batch)
    tri = (jnp.arange(tm_mid)[:, None] >= jnp.arange(tm_mid)[None, :]).astype(bf)
    bias = jnp.tile(b_forget.astype(jnp.float32), (1, LANES // FOX_HEADS))[:, None, :]
    w_in_t = jnp.swapaxes(w_in, 1, 2)
    conv_w8 = jnp.pad(conv_w.astype(jnp.float32), ((0, 0), (0, 8 - CONV_K), (0, 0)))
    row2 = lambda g: g.astype(jnp.float32)[:, None, :]
    g_q, g_kv = row2(g_q_lat), row2(g_kv_lat)
    g_mpre, g_mpost, g_fpre, g_fpost = row2(g_mix_pre), row2(g_mix_post), row2(g_ffn_pre), row2(g_ffn_post)
    srow = jnp.ones((1, N_CONVF_TILES * IN_TILE), jnp.float32)
    srow = srow.at[:, 3 * CONV_WIDTH:3 * CONV_WIDTH + FOX_HEADS * FOX_HEAD_DIM].set(FOX_QSCALE)
    lanes_per_tile = IN_TILE // LANES
    fq_col, fk_col, fv_col = (3 * lanes_per_tile, 4 * lanes_per_tile, 5 * lanes_per_tile)

    h, hn = _embed(x, meta, g_mpre, 0, tm_mid)

    for l in range(depth):
        gates = _wproj(hn, w_in_t, srow, l, COL_GATE, N_GATE_TILES, True, tm_big)
        convf = _wproj(hn, w_in_t, srow, l, COL_CONV, N_CONVF_TILES, False, tm_big)
        lat, small = _latproj(hn, w_in_t, l, tm_big)
        qm, qe, km, ke, v_a = _mla_latent(lat, small, g_q, g_kv, wq, wkv, cos_t, sin_t, l, lp, tm_mid)
        o_a = _attention(qm, qe, km, ke, v_a, batch, lp, MLA_HEADS)
        aq, ak = _fox_decay(small, bias, tri, selq, selk, l, batch, lp)
        o_c = _attention(convf, aq, convf, ak, convf, batch, lp, FOX_HEADS, qe_head_lanes=DECAY_LANES,
                         qm_col=fq_col, km_col=fk_col, v_col=fv_col)
        h, hn = _merge_out(o_a, convf, o_c, gates, conv_w8, wb, wo, h, g_mpost, g_fpre, l, lp, tm_small)
        act = _ffn_in(hn, w_ffn_in, l, tm_huge)
        gnext = g_mpre if l + 1 < depth else None
        h, hn = _ffn_out(act, wfo, h, g_fpost, gnext, l, tm_small)

    return h.reshape(batch, lp, D_MODEL)[:, N_META:]
```

```python
import functools
import math

import jax
import jax.numpy as jnp
from jax import lax
from jax.experimental import pallas as pl
from jax.experimental.pallas import tpu as pltpu

D_MODEL = 2048
N_META = 16
EPS = 1e-6
NEG_INF = -1e30
ROPE_THETA = 10000.0
MLA_HEADS = 8
MLA_Q_RANK = 512
MLA_KV_RANK = 512
MLA_NOPE = 128
MLA_ROPE = 64
MLA_V = 128
CONV_WIDTH = 1024
CONV_K = 3
FOX_HEADS = 8
FOX_HEAD_DIM = 128
N_BRANCH = 3
BRANCH_WIDTH = 1024
D_FF = 5632

LOG2E = 1.4426950408889634
MLA_QSCALE = (MLA_NOPE + MLA_ROPE) ** -0.5 * LOG2E
FOX_QSCALE = FOX_HEAD_DIM ** -0.5 * LOG2E

LANES = 128
MXU_N = 256
VMEM_LIMIT_BYTES = 56 * 1024 * 1024

BF16_ROWS = 16

IN_TILE = 1024
COL_KPE = MLA_Q_RANK + MLA_KV_RANK
COL_CONV = COL_KPE + MLA_ROPE
N_CONVF_TILES = (3 * CONV_WIDTH + 3 * FOX_HEADS * FOX_HEAD_DIM) // IN_TILE
COL_FLOG = COL_CONV + N_CONVF_TILES * IN_TILE
COL_GATE = COL_FLOG + FOX_HEADS
N_GATE_TILES = N_BRANCH * D_MODEL // IN_TILE
SMALL_W = 2 * LANES
CAST_ROWS = 256
FF_TILE = 512


def _row_tile(rows, target):
    return max(t for t in range(BF16_ROWS, target + 1, BF16_ROWS) if rows % t == 0)


def _cparams(semantics):
    return pltpu.CompilerParams(dimension_semantics=semantics, vmem_limit_bytes=VMEM_LIMIT_BYTES)


def _rms(x, g):
    ms = jnp.mean(x * x, axis=-1, keepdims=True)
    return x * lax.rsqrt(ms + EPS) * g


def _sigmoid(x):
    return 1.0 / (1.0 + jnp.exp(-x))


def _layer_spec(shape, l):
    zeros = (0,) * len(shape)
    return pl.BlockSpec((None,) + tuple(shape), lambda *_: (l,) + zeros)


def _prenorm_kernel(h_ref, g_ref, o_ref):
    o_ref[...] = _rms(h_ref[...], g_ref[...]).astype(o_ref.dtype)


def _prenorm(h, g, l, tm):
    tp = h.shape[0]
    return pl.pallas_call(
        _prenorm_kernel,
        out_shape=jax.ShapeDtypeStruct((tp, D_MODEL), jnp.bfloat16),
        grid=(tp // tm,),
        in_specs=[pl.BlockSpec((tm, D_MODEL), lambda i: (i, 0)), _layer_spec((1, D_MODEL), l)],
        out_specs=pl.BlockSpec((tm, D_MODEL), lambda i: (i, 0)),
        compiler_params=_cparams(("parallel",)),
        name="prenorm",
    )(h, g)


_NT = (((1,), (1,)), ((), ()))


def _wproj_kernel(x_ref, w_ref, srow_ref, o_ref, w_scr, *, sigmoid):
    @pl.when(pl.program_id(1) == 0)
    def _():
        for r in range(0, IN_TILE, CAST_ROWS):
            w_scr[r:r + CAST_ROWS, :] = w_ref[0, r:r + CAST_ROWS, :].astype(jnp.bfloat16)

    acc = lax.dot_general(x_ref[...], w_scr[...], _NT, preferred_element_type=jnp.float32)
    y = _sigmoid(acc) if sigmoid else acc * srow_ref[...]
    o_ref[...] = y.astype(o_ref.dtype)


def _wproj(xn, w_in_t, srow, l, col0, n_tiles, sigmoid, tm):
    tp = xn.shape[0]
    assert col0 % 8 == 0
    return pl.pallas_call(
        functools.partial(_wproj_kernel, sigmoid=sigmoid),
        out_shape=jax.ShapeDtypeStruct((tp, n_tiles * IN_TILE), jnp.bfloat16),
        grid=(n_tiles, tp // tm),
        in_specs=[pl.BlockSpec((tm, D_MODEL), lambda j, i: (i, 0)),
                  pl.BlockSpec((pl.Element(1), pl.Element(IN_TILE), pl.Element(D_MODEL)),
                               lambda j, i: (l, pl.multiple_of(col0 + j * IN_TILE, 8), 0)),
                  pl.BlockSpec((1, IN_TILE), lambda j, i: (0, j))],
        out_specs=pl.BlockSpec((tm, IN_TILE), lambda j, i: (i, j)),
        scratch_shapes=[pltpu.VMEM((IN_TILE, D_MODEL), jnp.bfloat16)],
        compiler_params=_cparams(("arbitrary", "arbitrary")),
        name="wproj_gates" if sigmoid else "wproj_convf",
    )(xn, w_in_t, srow)


def _latproj_kernel(x_ref, wl_ref, wk_ref, wf_ref, lat_ref, small_ref, wl_scr, ws_scr):
    @pl.when(pl.program_id(0) == 0)
    def _():
        for r in range(0, IN_TILE, CAST_ROWS):
            wl_scr[r:r + CAST_ROWS, :] = wl_ref[r:r + CAST_ROWS, :].astype(jnp.bfloat16)
        kpe = wk_ref[...]
        half = MLA_ROPE // 2
        fill = jnp.zeros((SMALL_W - 2 * MLA_ROPE - FOX_HEADS, D_MODEL), jnp.float32)
        ws = jnp.concatenate([kpe, -kpe[half:], kpe[:half], wf_ref[...], fill], axis=0)
        ws_scr[...] = ws.astype(jnp.bfloat16)

    x = x_ref[...]
    lat_ref[...] = lax.dot_general(x, wl_scr[...], _NT, preferred_element_type=jnp.float32).astype(lat_ref.dtype)
    small_ref[...] = lax.dot_general(x, ws_scr[...], _NT, preferred_element_type=jnp.float32)


def _latproj(xn, w_in_t, l, tm):
    tp = xn.shape[0]
    return pl.pallas_call(
        _latproj_kernel,
        out_shape=(jax.ShapeDtypeStruct((tp, IN_TILE), jnp.bfloat16),
                   jax.ShapeDtypeStruct((tp, SMALL_W), jnp.float32)),
        grid=(tp // tm,),
        in_specs=[pl.BlockSpec((tm, D_MODEL), lambda i: (i, 0)),
                  pl.BlockSpec((None, IN_TILE, D_MODEL), lambda i: (l, 0, 0), pipeline_mode=pl.Buffered(1)),
                  pl.BlockSpec((None, MLA_ROPE, D_MODEL), lambda i: (l, COL_KPE // MLA_ROPE, 0)),
                  pl.BlockSpec((None, FOX_HEADS, D_MODEL), lambda i: (l, COL_FLOG // FOX_HEADS, 0))],
        out_specs=(pl.BlockSpec((tm, IN_TILE), lambda i: (i, 0)),
                   pl.BlockSpec((tm, SMALL_W), lambda i: (i, 0))),
        scratch_shapes=[pltpu.VMEM((IN_TILE, D_MODEL), jnp.bfloat16),
                        pltpu.VMEM((SMALL_W, D_MODEL), jnp.bfloat16)],
        compiler_params=_cparams(("arbitrary",)),
        name="latproj",
    )(xn, w_in_t, w_in_t, w_in_t)


def _rope_pair(v, cos_t, sin_t):
    return v * cos_t + pltpu.roll(v, 64, 1) * sin_t


def _mla_latent_kernel(lat_ref, small_ref, gq_ref, gkv_ref, wq_ref, wkv_ref, cos_ref, sin_ref,
                       qm_ref, qe_ref, km_ref, ke_ref, v_ref):
    lat = lat_ref[...].astype(jnp.float32)
    cos_t = cos_ref[...]
    sin_t = sin_ref[...]
    cq = _rms(lat[:, :MLA_Q_RANK], gq_ref[...]).astype(jnp.bfloat16)
    ckv = _rms(lat[:, MLA_Q_RANK:], gkv_ref[...]).astype(jnp.bfloat16)
    q = jnp.dot(cq, wq_ref[...], preferred_element_type=jnp.float32)
    for h in range(MLA_HEADS):
        base = h * 2 * LANES
        qm_ref[:, h * LANES:(h + 1) * LANES] = (q[:, base:base + LANES] * MLA_QSCALE).astype(qm_ref.dtype)
        roped = _rope_pair(q[:, base + LANES:base + 2 * LANES], cos_t, sin_t)
        qe_ref[:, h * LANES:(h + 1) * LANES] = (roped * MLA_QSCALE).astype(qe_ref.dtype)
    kv = jnp.dot(ckv, wkv_ref[...], preferred_element_type=jnp.float32)
    km_ref[...] = kv[:, :MLA_HEADS * MLA_NOPE].astype(km_ref.dtype)
    v_ref[...] = kv[:, MLA_HEADS * MLA_NOPE:].astype(v_ref.dtype)
    ke_ref[...] = _rope_pair(small_ref[...], cos_t, sin_t).astype(ke_ref.dtype)


def _mla_latent(lat, small, gq, gkv, wq, wkv, cos_t, sin_t, l, lp, tm):
    tp = lat.shape[0]
    per_batch = lp // tm
    hw = MLA_HEADS * LANES
    bf = jnp.bfloat16
    row = lambda i: (i, 0)
    return pl.pallas_call(
        _mla_latent_kernel,
        out_shape=(jax.ShapeDtypeStruct((tp, hw), bf), jax.ShapeDtypeStruct((tp, hw), bf),
                   jax.ShapeDtypeStruct((tp, hw), bf), jax.ShapeDtypeStruct((tp, LANES), bf),
                   jax.ShapeDtypeStruct((tp, hw), bf)),
        grid=(tp // tm,),
        in_specs=[pl.BlockSpec((tm, IN_TILE), row),
                  pl.BlockSpec((tm, LANES), row),
                  _layer_spec((1, MLA_Q_RANK), l),
                  _layer_spec((1, MLA_KV_RANK), l),
                  _layer_spec((MLA_Q_RANK, 2 * hw), l),
                  _layer_spec((MLA_KV_RANK, 2 * hw), l),
                  pl.BlockSpec((tm, LANES), lambda i: (i % per_batch, 0)),
                  pl.BlockSpec((tm, LANES), lambda i: (i % per_batch, 0))],
        out_specs=(pl.BlockSpec((tm, hw), row), pl.BlockSpec((tm, hw), row),
                   pl.BlockSpec((tm, hw), row), pl.BlockSpec((tm, LANES), row),
                   pl.BlockSpec((tm, hw), row)),
        compiler_params=_cparams(("parallel",)),
        name="mla_latent",
    )(lat, small, gq, gkv, wq, wkv, cos_t, sin_t)


def _split3(x):
    hi = x.astype(jnp.bfloat16)
    r1 = x - hi.astype(jnp.float32)
    mid = r1.astype(jnp.bfloat16)
    lo = (r1 - mid.astype(jnp.float32)).astype(jnp.bfloat16)
    return hi, mid, lo


def _fox_decay_kernel(fl_ref, bias_ref, tri_ref, selq_ref, selk_ref, aq_ref, ak_ref, carry_ref):
    blk = pl.program_id(0)

    @pl.when(blk == 0)
    def _():
        carry_ref[...] = jnp.zeros_like(carry_ref)

    tri = tri_ref[...]
    rows_per_blk = tri.shape[0]
    lane = lax.broadcasted_iota(jnp.int32, aq_ref.shape[1:], 1)
    z = fl_ref[0]
    for b in range(1, fl_ref.shape[0]):
        lane0 = b * FOX_HEADS
        z = jnp.where((lane >= lane0) & (lane < lane0 + FOX_HEADS), pltpu.roll(fl_ref[b], lane0, 1), z)
    z = z + bias_ref[...]
    log_f = (jnp.minimum(z, 0.0) - jnp.log1p(jnp.exp(-jnp.abs(z)))) * LOG2E
    c = carry_ref[0:1, :]
    for part in _split3(log_f):
        c = c + jnp.dot(tri, part, preferred_element_type=jnp.float32)
    carry_ref[...] = jnp.broadcast_to(c[rows_per_blk - 1:rows_per_blk, :], carry_ref.shape)
    parts = jnp.concatenate(_split3(c), axis=1)
    lane = lane % DECAY_LANES
    for b in range(fl_ref.shape[0]):
        aq = jnp.dot(parts, selq_ref[b], preferred_element_type=jnp.float32)
        ak = jnp.dot(parts, selk_ref[b], preferred_element_type=jnp.float32)
        aq_ref[b] = jnp.where((lane >= 3) & (lane < 6), 1.0, aq).astype(aq_ref.dtype)
        ak_ref[b] = jnp.where(lane < 3, 1.0, ak).astype(ak_ref.dtype)


def _fox_decay(small, bias, tri, selq, selk, l, batch, lp):
    tp = small.shape[0]
    blk_rows = tri.shape[0]
    nblk = lp // blk_rows
    hw = FOX_HEADS * DECAY_LANES
    assert batch * FOX_HEADS <= LANES and selq.shape == (batch, 3 * LANES, hw)
    const = lambda i: (0, 0)
    const3 = lambda i: (0, 0, 0)
    out = jax.ShapeDtypeStruct((batch, lp, hw), jnp.bfloat16)
    aq, ak = pl.pallas_call(
        _fox_decay_kernel,
        out_shape=(out, out),
        grid=(nblk,),
        in_specs=[pl.BlockSpec((batch, blk_rows, LANES), lambda i: (0, i, 1)),
                  _layer_spec((1, LANES), l),
                  pl.BlockSpec((blk_rows, blk_rows), const),
                  pl.BlockSpec((batch, 3 * LANES, hw), const3),
                  pl.BlockSpec((batch, 3 * LANES, hw), const3)],
        out_specs=(pl.BlockSpec((batch, blk_rows, hw), lambda i: (0, i, 0)),
                   pl.BlockSpec((batch, blk_rows, hw), lambda i: (0, i, 0))),
        scratch_shapes=[pltpu.VMEM((8, LANES), jnp.float32)],
        compiler_params=_cparams(("arbitrary",)),
        name="fox_decay",
    )(small.reshape(batch, lp, SMALL_W), bias, tri, selq, selk)
    return aq.reshape(tp, hw), ak.reshape(tp, hw)


Q_TILE = 256
HEADS_PER_STEP = 2
DECAY_LANES = LANES // FOX_HEADS


def _attn_scores(qc_ref, kc_ref, r0, tq):
    q = qc_ref[r0:r0 + tq, :]

    def scores(k0, k1):
        return lax.dot_general(q, kc_ref[k0:k1, :], _NT, preferred_element_type=jnp.float32)

    pieces = []
    if r0 > 0:
        pieces.append((scores(0, N_META), 0, N_META))
        if r0 > N_META:
            pieces.append((scores(N_META, r0), N_META, r0))
    causal = (lax.broadcasted_iota(jnp.int32, (tq, tq), 1)
              <= lax.broadcasted_iota(jnp.int32, (tq, tq), 0))
    pieces.append((jnp.where(causal, scores(r0, r0 + tq), NEG_INF), r0, r0 + tq))
    return pieces


def _attn_softmax_pv(pieces, v_ref, vsl):
    m = None
    for s, _, _ in pieces:
        pm = jnp.max(s, axis=1, keepdims=True)
        m = pm if m is None else jnp.maximum(m, pm)
    l = None
    acc = None
    for s, k0, k1 in pieces:
        p = jnp.exp2(s - m)
        ps = jnp.sum(p, axis=1, keepdims=True)
        pv = jnp.dot(p.astype(jnp.bfloat16), v_ref[k0:k1, vsl], preferred_element_type=jnp.float32)
        l = ps if l is None else l + ps
        acc = pv if acc is None else acc + pv
    return acc / l


def _attn_kernel(qm_ref, qe_ref, km_ref, ke_ref, v_ref, o_ref, qc_ref, kc_ref, *, qe_head_lanes):
    lp = o_ref.shape[0]
    tiles = [(0, N_META)] + [(r0, Q_TILE) for r0 in range(N_META, lp, Q_TILE)]
    jobs = []
    for hh in range(HEADS_PER_STEP):
        sl = slice(hh * LANES, (hh + 1) * LANES)
        if qe_head_lanes:
            qe = qe_ref[...]
            lane0 = (pl.program_id(1) * HEADS_PER_STEP + hh) * qe_head_lanes
            lane = lax.broadcasted_iota(jnp.int32, qe.shape, 1)
            qe = jnp.where((lane >= lane0) & (lane < lane0 + qe_head_lanes), qe, jnp.zeros_like(qe))
        else:
            qe = qe_ref[:, sl]
        qc_ref[hh, :, :LANES] = qm_ref[:, sl]
        qc_ref[hh, :, LANES:] = qe
        kc_ref[hh, :, :LANES] = km_ref[:, sl]
        kc_ref[hh, :, LANES:] = ke_ref[...]
        jobs += [(hh, sl, r0, tq) for r0, tq in tiles]
    scores = lambda job: _attn_scores(qc_ref.at[job[0]], kc_ref.at[job[0]], job[2], job[3])
    pieces = scores(jobs[0])
    for t, (hh, sl, r0, tq) in enumerate(jobs):
        nxt = scores(jobs[t + 1]) if t + 1 < len(jobs) else None
        o_ref[r0:r0 + tq, sl] = _attn_softmax_pv(pieces, v_ref, sl).astype(o_ref.dtype)
        pieces = nxt


def _attention(qm, qe, km, ke, v, batch, lp, n_heads, qe_head_lanes=0, qm_col=0, km_col=0, v_col=0):
    assert (lp - N_META) % Q_TILE == 0
    hps = HEADS_PER_STEP
    assert n_heads % hps == 0 and qm_col % hps == 0 and km_col % hps == 0 and v_col % hps == 0
    tp = qm.shape[0]
    pair = lambda b, g: (b, g)
    at = lambda col: (lambda b, g: (b, col // hps + g))
    shared = lambda b, g: (b, 0)
    wide = lambda m: pl.BlockSpec((lp, hps * LANES), m)
    narrow = lambda m: pl.BlockSpec((lp, LANES), m)
    return pl.pallas_call(
        functools.partial(_attn_kernel, qe_head_lanes=qe_head_lanes),
        out_shape=jax.ShapeDtypeStruct((tp, n_heads * LANES), jnp.bfloat16),
        grid=(batch, n_heads // hps),
        in_specs=[wide(at(qm_col)), narrow(shared) if qe_head_lanes else wide(pair),
                  wide(at(km_col)), narrow(shared), wide(at(v_col))],
        out_specs=wide(pair),
        scratch_shapes=[pltpu.VMEM((hps, lp, 2 * LANES), jnp.bfloat16),
                        pltpu.VMEM((hps, lp, 2 * LANES), jnp.bfloat16)],
        compiler_params=_cparams(("parallel", "parallel")),
        name="attention",
    )(qm, qe, km, ke, v)


def _gated_conv_rows(b_ref, c_ref, x_ref, w_ref, tail_ref, rows_per_batch):
    i = pl.program_id(0)
    tm = b_ref.shape[0]

    @pl.when(i == 0)
    def _():
        tail_ref[...] = jnp.zeros_like(tail_ref)

    u = c_ref[...].astype(jnp.float32) * x_ref[...].astype(jnp.float32)
    rows = lax.broadcasted_iota(jnp.int32, u.shape, 0)
    start = (rows_per_batch - (i * tm) % rows_per_batch) % rows_per_batch
    prev1 = tail_ref[7:8, :]
    prev2 = tail_ref[6:7, :]
    u1 = jnp.where(rows == 0, prev1, pltpu.roll(u, 1, 0))
    u1 = jnp.where(rows == start, 0.0, u1)
    u2 = jnp.where(rows == 0, prev2, jnp.where(rows == 1, prev1, pltpu.roll(u, 2, 0)))
    u2 = jnp.where((rows == start) | (rows == start + 1), 0.0, u2)
    tail_ref[...] = u[tm - 8:tm, :]
    w = w_ref[...]
    return b_ref[...].astype(jnp.float32) * (w[2:3, :] * u + w[1:2, :] * u1 + w[0:1, :] * u2)


def _project_residual(a, w_ref, h_ref, gpost_ref, gnext_ref, ho_ref, hn_ref):
    y = jnp.dot(a, w_ref[...], preferred_element_type=jnp.float32)
    h_new = h_ref[...] + _rms(y, gpost_ref[...])
    ho_ref[...] = h_new
    if hn_ref is not None:
        hn_ref[...] = _rms(h_new, gnext_ref[...]).astype(hn_ref.dtype)


def _merge_out_kernel(oa_ref, cb_ref, cc_ref, cx_ref, oc_ref, gate_ref, cw_ref, wb_ref, wo_ref, h_ref,
                      gpost_ref, gnext_ref, ho_ref, hn_ref, tail_ref, *, rows_per_batch):
    o_b = _gated_conv_rows(cb_ref, cc_ref, cx_ref, cw_ref, tail_ref, rows_per_batch).astype(jnp.bfloat16)
    merged = None
    for n, o in enumerate((oa_ref[...], o_b, oc_ref[...])):
        y = jnp.dot(o, wb_ref[n], preferred_element_type=jnp.float32)
        term = gate_ref[:, n * D_MODEL:(n + 1) * D_MODEL].astype(jnp.float32) * y
        merged = term if merged is None else merged + term
    _project_residual(merged.astype(jnp.bfloat16), wo_ref, h_ref, gpost_ref, gnext_ref, ho_ref, hn_ref)


def _merge_out(o_a, convf, o_c, gates, conv_w, wb, wo, h, gpost, gnext, l, lp, tm):
    tp = h.shape[0]
    assert math.gcd(tm, lp) > 1
    row = lambda i: (i, 0)
    single = pl.Buffered(1)
    return pl.pallas_call(
        functools.partial(_merge_out_kernel, rows_per_batch=lp),
        out_shape=(jax.ShapeDtypeStruct((tp, D_MODEL), jnp.float32),
                   jax.ShapeDtypeStruct((tp, D_MODEL), jnp.bfloat16)),
        grid=(tp // tm,),
        in_specs=[pl.BlockSpec((tm, BRANCH_WIDTH), row),
                  pl.BlockSpec((tm, IN_TILE), lambda i: (i, 0)),
                  pl.BlockSpec((tm, IN_TILE), lambda i: (i, 1)),
                  pl.BlockSpec((tm, IN_TILE), lambda i: (i, 2)),
                  pl.BlockSpec((tm, BRANCH_WIDTH), row),
                  pl.BlockSpec((tm, N_BRANCH * D_MODEL), row),
                  _layer_spec((8, CONV_WIDTH), l),
                  pl.BlockSpec((None, N_BRANCH, BRANCH_WIDTH, D_MODEL), lambda i: (l, 0, 0, 0),
                               pipeline_mode=single),
                  pl.BlockSpec((None, D_MODEL, D_MODEL), lambda i: (l, 0, 0), pipeline_mode=single),
                  pl.BlockSpec((tm, D_MODEL), row),
                  _layer_spec((1, D_MODEL), l),
                  _layer_spec((1, D_MODEL), l)],
        out_specs=(pl.BlockSpec((tm, D_MODEL), row), pl.BlockSpec((tm, D_MODEL), row)),
        scratch_shapes=[pltpu.VMEM((8, CONV_WIDTH), jnp.float32)],
        compiler_params=_cparams(("arbitrary",)),
        name="merge_out",
    )(o_a, convf, convf, convf, o_c, gates, conv_w, wb, wo, h, gpost, gnext)


def _ffn_in_kernel(x_ref, wg_ref, wu_ref, o_ref, wg_scr, wu_scr):
    @pl.when(pl.program_id(1) == 0)
    def _():
        for r in range(0, D_MODEL, CAST_ROWS):
            wg_scr[r:r + CAST_ROWS, :] = wg_ref[r:r + CAST_ROWS, :].astype(jnp.bfloat16)
            wu_scr[r:r + CAST_ROWS, :] = wu_ref[r:r + CAST_ROWS, :].astype(jnp.bfloat16)

    x = x_ref[...]
    for c in range(0, FF_TILE, MXU_N):
        g = jnp.dot(x, wg_scr[:, c:c + MXU_N], preferred_element_type=jnp.float32)
        u = jnp.dot(x, wu_scr[:, c:c + MXU_N], preferred_element_type=jnp.float32)
        o_ref[:, c:c + MXU_N] = (g * _sigmoid(g) * u).astype(o_ref.dtype)


def _ffn_in(xn, w, l, tm):
    tp = xn.shape[0]
    n_tiles = D_FF // FF_TILE
    return pl.pallas_call(
        _ffn_in_kernel,
        out_shape=jax.ShapeDtypeStruct((tp, D_FF), jnp.bfloat16),
        grid=(n_tiles, tp // tm),
        in_specs=[pl.BlockSpec((tm, D_MODEL), lambda j, i: (i, 0)),
                  pl.BlockSpec((None, D_MODEL, FF_TILE), lambda j, i: (l, 0, j)),
                  pl.BlockSpec((None, D_MODEL, FF_TILE), lambda j, i: (l, 0, j + n_tiles))],
        out_specs=pl.BlockSpec((tm, FF_TILE), lambda j, i: (i, j)),
        scratch_shapes=[pltpu.VMEM((D_MODEL, FF_TILE), jnp.bfloat16),
                        pltpu.VMEM((D_MODEL, FF_TILE), jnp.bfloat16)],
        compiler_params=_cparams(("arbitrary", "arbitrary")),
        name="ffn_in",
    )(xn, w, w)


def _ffn_out_kernel(a_ref, w_ref, h_ref, gpost_ref, gnext_ref, ho_ref, hn_ref):
    _project_residual(a_ref[...], w_ref, h_ref, gpost_ref, gnext_ref, ho_ref, hn_ref)


def _ffn_out_last_kernel(a_ref, w_ref, h_ref, gpost_ref, ho_ref):
    _project_residual(a_ref[...], w_ref, h_ref, gpost_ref, None, ho_ref, None)


def _ffn_out(act, w, h, gpost, gnext, l, tm):
    tp = h.shape[0]
    row = lambda i: (i, 0)
    in_specs = [pl.BlockSpec((tm, D_FF), row),
                pl.BlockSpec((None, D_FF, D_MODEL), lambda i: (l, 0, 0), pipeline_mode=pl.Buffered(1)),
                pl.BlockSpec((tm, D_MODEL), row),
                _layer_spec((1, D_MODEL), l)]
    h_shape = jax.ShapeDtypeStruct((tp, D_MODEL), jnp.float32)
    h_spec = pl.BlockSpec((tm, D_MODEL), row)
    if gnext is None:
        return pl.pallas_call(
            _ffn_out_last_kernel, out_shape=h_shape, grid=(tp // tm,), in_specs=in_specs,
            out_specs=h_spec, compiler_params=_cparams(("parallel",)), name="ffn_out_last",
        )(act, w, h, gpost), None
    return pl.pallas_call(
        _ffn_out_kernel,
        out_shape=(h_shape, jax.ShapeDtypeStruct((tp, D_MODEL), jnp.bfloat16)),
        grid=(tp // tm,),
        in_specs=in_specs + [_layer_spec((1, D_MODEL), l + 1)],
        out_specs=(h_spec, pl.BlockSpec((tm, D_MODEL), row)),
        compiler_params=_cparams(("parallel",)),
        name="ffn_out",
    )(act, w, h, gpost, gnext)


def _prep_small_weights(w_uq, w_ukv):
    bf = jnp.bfloat16
    depth = w_uq.shape[0]
    half = MLA_ROPE // 2
    wq = w_uq.reshape(depth, MLA_Q_RANK, MLA_HEADS, MLA_NOPE + MLA_ROPE)
    pe = wq[..., MLA_NOPE:]
    rot = jnp.concatenate([-pe[..., half:], pe[..., :half]], axis=-1)
    wq = jnp.concatenate([wq[..., :MLA_NOPE], pe, rot], axis=-1)
    wq = wq.reshape(depth, MLA_Q_RANK, MLA_HEADS * 2 * LANES).astype(bf)
    wkv = w_ukv.reshape(depth, MLA_KV_RANK, MLA_HEADS, MLA_NOPE + MLA_V)
    wkv = jnp.concatenate([wkv[..., :MLA_NOPE].reshape(depth, MLA_KV_RANK, -1),
                           wkv[..., MLA_NOPE:].reshape(depth, MLA_KV_RANK, -1)], axis=-1).astype(bf)
    return wq, wkv


def _rope_tables(lp):
    inv_freq = 1.0 / (ROPE_THETA ** (jnp.arange(0, MLA_ROPE, 2, dtype=jnp.float32) / MLA_ROPE))
    pos = jnp.arange(lp, dtype=jnp.float32)
    ang = pos[:, None] * inv_freq[None, :]
    zeros = jnp.zeros((lp, LANES - MLA_ROPE), jnp.float32)
    cos_t = jnp.concatenate([jnp.cos(ang), jnp.cos(ang), zeros], axis=1)
    sin_t = jnp.concatenate([jnp.sin(ang), jnp.sin(ang), zeros], axis=1)
    return cos_t, sin_t


def _decay_selectors(batch):
    src = jnp.arange(3 * LANES)
    dst = jnp.arange(FOX_HEADS * DECAY_LANES)
    part, lane_src = src // LANES, src % LANES
    head_dst, lane_dst = dst // DECAY_LANES, dst % DECAY_LANES
    b = jnp.arange(batch)[:, None, None]
    same_head = lane_src[None, :, None] == b * FOX_HEADS + head_dst[None, None, :]
    selq = (same_head & (lane_dst[None, None, :] == part[None, :, None])).astype(jnp.bfloat16)
    selk = -(same_head & (lane_dst[None, None, :] == part[None, :, None] + 3)).astype(jnp.bfloat16)
    return selq, selk


def kernel(x, meta, w_in, b_forget, g_q_lat, g_kv_lat, w_uq, w_ukv, conv_w, w_branch, w_out,
           w_ffn_in, w_ffn_out, g_mix_pre, g_mix_post, g_ffn_pre, g_ffn_post):
    batch, seq, _ = x.shape
    depth = w_in.shape[0]
    lp = N_META + seq
    tp = batch * lp
    tm_huge = _row_tile(tp, 2100)
    tm_big = _row_tile(tp, 1400)
    tm_mid = _row_tile(lp, 700)
    tm_small = _row_tile(tp, 288)

    bf = jnp.bfloat16
    wq, wkv = _prep_small_weights(w_uq, w_ukv)
    wb, wo, wfo = w_branch.astype(bf), w_out.astype(bf), w_ffn_out.astype(bf)
    cos_t, sin_t = _rope_tables(lp)
    selq, selk = _decay_selectors(batch)
    tri = (jnp.arange(tm_mid)[:, None] >= jnp.arange(tm_mid)[None, :]).astype(bf)
    bias = jnp.tile(b_forget.astype(jnp.float32), (1, LANES // FOX_HEADS))[:, None, :]
    w_in_t = jnp.swapaxes(w_in, 1, 2)
    conv_w8 = jnp.pad(conv_w.astype(jnp.float32), ((0, 0), (0, 8 - CONV_K), (0, 0)))
    row2 = lambda g: g.astype(jnp.float32)[:, None, :]
    g_q, g_kv = row2(g_q_lat), row2(g_kv_lat)
    g_mpre, g_mpost, g_fpre, g_fpost = row2(g_mix_pre), row2(g_mix_post), row2(g_ffn_pre), row2(g_ffn_post)
    srow = jnp.ones((1, N_CONVF_TILES * IN_TILE), jnp.float32)
    srow = srow.at[:, 3 * CONV_WIDTH:3 * CONV_WIDTH + FOX_HEADS * FOX_HEAD_DIM].set(FOX_QSCALE)
    lanes_per_tile = IN_TILE // LANES
    fq_col, fk_col, fv_col = (3 * lanes_per_tile, 4 * lanes_per_tile, 5 * lanes_per_tile)

    h = jnp.concatenate([jnp.broadcast_to(meta[None].astype(x.dtype), (batch, N_META, D_MODEL)), x], axis=1)
    h = h.reshape(tp, D_MODEL)
    hn = _prenorm(h, g_mpre, 0, tm_mid)

    for l in range(depth):
        gates = _wproj(hn, w_in_t, srow, l, COL_GATE, N_GATE_TILES, True, tm_big)
        convf = _wproj(hn, w_in_t, srow, l, COL_CONV, N_CONVF_TILES, False, tm_big)
        lat, small = _latproj(hn, w_in_t, l, tm_big)
        qm, qe, km, ke, v_a = _mla_latent(lat, small, g_q, g_kv, wq, wkv, cos_t, sin_t, l, lp, tm_mid)
        o_a = _attention(qm, qe, km, ke, v_a, batch, lp, MLA_HEADS)
        aq, ak = _fox_decay(small, bias, tri, selq, selk, l, batch, lp)
        o_c = _attention(convf, aq, convf, ak, convf, batch, lp, FOX_HEADS, qe_head_lanes=DECAY_LANES,
                         qm_col=fq_col, km_col=fk_col, v_col=fv_col)
        h, hn = _merge_out(o_a, convf, o_c, gates, conv_w8, wb, wo, h, g_mpost, g_fpre, l, lp, tm_small)
        act = _ffn_in(hn, w_ffn_in, l, tm_huge)
        gnext = g_mpre if l + 1 < depth else None
        h, hn = _ffn_out(act, wfo, h, g_fpost, gnext, l, tm_small)

    return h.reshape(batch, lp, D_MODEL)[:, N_META:]
```
